```python
import math
import jax, jax.numpy as jnp
from jax import lax
import numpy as np


D_MODEL = 1024
BATCH = 1
SEQ = 16384
DEPTH = 4

GRID_W = 64
ROPE_THETA = 10000.0
EPS = 1e-6
NEG_INF = -1e30
Q_BLOCK = 128

MLA_HEADS = 4
MLA_Q_RANK = 192
MLA_KV_RANK = 128
MLA_NOPE_DIM = 64
MLA_ROPE_DIM = 32
MLA_V_DIM = 64
DIFF_HEADS = 4
DIFF_QK_DIM = 32
DIFF_V_DIM = 64
SWA_Q_HEADS = 4
SWA_KV_HEADS = 2
SWA_GROUP = SWA_Q_HEADS // SWA_KV_HEADS
SWA_HEAD_DIM = 64
WINDOW = 128
SWA_BLOCK = 128
NA_HEADS = 4
NA_HEAD_DIM = 64
NA_ROWS_MAX = 8
NA_COLS = 16
D_FF = 4 * D_MODEL

A_OUT = MLA_HEADS * MLA_V_DIM
B_OUT = DIFF_HEADS * DIFF_V_DIM
C_OUT = SWA_Q_HEADS * SWA_HEAD_DIM
D_OUT = NA_HEADS * NA_HEAD_DIM
MIX_WIDTH = A_OUT + B_OUT + C_OUT + D_OUT

IN_SPLIT_WIDTHS = (
    MLA_Q_RANK, MLA_KV_RANK, MLA_ROPE_DIM,
    DIFF_HEADS * 2 * DIFF_QK_DIM, DIFF_HEADS * 2 * DIFF_QK_DIM, B_OUT,
    SWA_Q_HEADS * SWA_HEAD_DIM, SWA_KV_HEADS * SWA_HEAD_DIM, SWA_KV_HEADS * SWA_HEAD_DIM,
    NA_HEADS * NA_HEAD_DIM, NA_HEADS * NA_HEAD_DIM, NA_HEADS * NA_HEAD_DIM,
)
IN_COLS = sum(IN_SPLIT_WIDTHS)

kernel_name = 'hybrid_parallel_head_group_encoder'


def _split_points():
    pts, acc = [], 0
    for w in IN_SPLIT_WIDTHS[:-1]:
        acc += w
        pts.append(acc)
    return pts


def rms_norm(x, g):
    xf = x.astype(jnp.float32)
    y = xf * lax.rsqrt(jnp.mean(xf * xf, axis=-1, keepdims=True) + EPS)
    return (y * g.astype(jnp.float32)).astype(x.dtype)


def rope_table(seq_len, dim):
    inv = 1.0 / (ROPE_THETA ** (jnp.arange(0, dim, 2, dtype=jnp.float32) / dim))
    ang = jnp.arange(seq_len, dtype=jnp.float32)[:, None] * inv[None, :]
    return jnp.cos(ang), jnp.sin(ang)


def apply_rope(x, cos, sin):
    shp = (1, x.shape[1]) + (1,) * (x.ndim - 3) + (cos.shape[-1],)
    c, s = cos.reshape(shp), sin.reshape(shp)
    xf = x.astype(jnp.float32)
    half = x.shape[-1] // 2
    x1, x2 = xf[..., :half], xf[..., half:]
    return jnp.concatenate([x1 * c - x2 * s, x2 * c + x1 * s], axis=-1).astype(x.dtype)


def dense_block_attention(q, k, v, scale):
    B, S, H, dq = q.shape
    nb = S // Q_BLOCK
    qb = jnp.moveaxis(q.reshape(B, nb, Q_BLOCK, H, dq), 1, 0)

    def one_block(qi):
        s = jnp.einsum('bqhd,bkhd->bhqk', qi, k, preferred_element_type=jnp.float32) * scale
        p = jax.nn.softmax(s, axis=-1)
        return jnp.einsum('bhqk,bkhd->bqhd', p.astype(v.dtype), v)

    o = lax.map(one_block, qb)
    return jnp.moveaxis(o, 0, 1).reshape(B, S, H, v.shape[-1])


def mla_mixer(cq, ckv, kr, q_norm_g, w_uq, kv_norm_g, w_uk, w_uv, cos, sin):
    B, S, _ = cq.shape
    q = (rms_norm(cq, q_norm_g) @ w_uq).reshape(B, S, MLA_HEADS, MLA_NOPE_DIM + MLA_ROPE_DIM)
    q = jnp.concatenate([q[..., :MLA_NOPE_DIM], apply_rope(q[..., MLA_NOPE_DIM:], cos, sin)], axis=-1)
    c = rms_norm(ckv, kv_norm_g)
    k_nope = (c @ w_uk).reshape(B, S, MLA_HEADS, MLA_NOPE_DIM)
    v = (c @ w_uv).reshape(B, S, MLA_HEADS, MLA_V_DIM)
    k_pe = apply_rope(kr[:, :, None, :], cos, sin)
    k = jnp.concatenate([k_nope, jnp.broadcast_to(k_pe, (B, S, MLA_HEADS, MLA_ROPE_DIM))], axis=-1)
    o = dense_block_attention(q, k, v, (MLA_NOPE_DIM + MLA_ROPE_DIM) ** -0.5)
    return o.reshape(B, S, A_OUT)


def diff_mixer(q, k, v, lq1, lk1, lq2, lk2, subln_g, lambda_init, cos, sin):
    B, S, _ = q.shape
    q = apply_rope(q.reshape(B, S, DIFF_HEADS, 2, DIFF_QK_DIM), cos, sin)
    k = apply_rope(k.reshape(B, S, DIFF_HEADS, 2, DIFF_QK_DIM), cos, sin)
    v = v.reshape(B, S, DIFF_HEADS, DIFF_V_DIM)
    f32 = jnp.float32
    lam = (jnp.exp(jnp.sum(lq1.astype(f32) * lk1.astype(f32)))
           - jnp.exp(jnp.sum(lq2.astype(f32) * lk2.astype(f32))) + lambda_init)
    scale = DIFF_QK_DIM ** -0.5
    nb = S // Q_BLOCK
    qb = jnp.moveaxis(q.reshape(B, nb, Q_BLOCK, DIFF_HEADS, 2, DIFF_QK_DIM), 1, 0)

    def one_block(qi):
        s = jnp.einsum('bqhcd,bkhcd->bhcqk', qi, k, preferred_element_type=f32) * scale
        p = jax.nn.softmax(s, axis=-1)
        w = p[:, :, 0] - lam * p[:, :, 1]
        return jnp.einsum('bhqk,bkhd->bqhd', w.astype(v.dtype), v)

    o = jnp.moveaxis(lax.map(one_block, qb), 0, 1).reshape(B, S, DIFF_HEADS, DIFF_V_DIM)
    o = rms_norm(o, subln_g) * (1.0 - lambda_init)
    return o.reshape(B, S, B_OUT)


def swa_mixer(q, k, v, sinks, cos, sin):
    B, S, _ = q.shape
    q = apply_rope(q.reshape(B, S, SWA_KV_HEADS, SWA_GROUP, SWA_HEAD_DIM), cos, sin)
    k = apply_rope(k.reshape(B, S, SWA_KV_HEADS, SWA_HEAD_DIM), cos, sin)
    v = v.reshape(B, S, SWA_KV_HEADS, SWA_HEAD_DIM)
    nb = S // SWA_BLOCK
    qb = q.reshape(B, nb, SWA_BLOCK, SWA_KV_HEADS, SWA_GROUP, SWA_HEAD_DIM)

    def band(t):
        tp = jnp.pad(t, ((0, 0), (SWA_BLOCK, SWA_BLOCK), (0, 0), (0, 0)))
        tp = tp.reshape(B, nb + 2, SWA_BLOCK, SWA_KV_HEADS, SWA_HEAD_DIM)
        return jnp.concatenate([tp[:, :-2], tp[:, 1:-1], tp[:, 2:]], axis=2)

    kb, vb = band(k), band(v)
    s = jnp.einsum('bnqkgd,bnjkd->bnkgqj', qb, kb,
                   preferred_element_type=jnp.float32) * (SWA_HEAD_DIM ** -0.5)
    qi = jnp.arange(SWA_BLOCK)[:, None]
    kj = jnp.arange(3 * SWA_BLOCK)[None, :]
    rel = kj - SWA_BLOCK - qi
    kabs = jnp.arange(nb)[:, None] * SWA_BLOCK - SWA_BLOCK + jnp.arange(3 * SWA_BLOCK)[None, :]
    valid = (jnp.abs(rel) <= WINDOW)[None] & ((kabs >= 0) & (kabs < S))[:, None, :]
    s = jnp.where(valid[None, :, None, None], s, NEG_INF)
    sink = jnp.broadcast_to(sinks.astype(jnp.float32).reshape(1, 1, SWA_KV_HEADS, SWA_GROUP, 1, 1),
                            s.shape[:-1] + (1,))
    p = jax.nn.softmax(jnp.concatenate([s, sink], axis=-1), axis=-1)[..., :-1]
    o = jnp.einsum('bnkgqj,bnjkd->bnqkgd', p.astype(v.dtype), vb)
    return o.reshape(B, S, C_OUT)


def na_mixer(q, k, v, rpb):
    B, S, _ = q.shape
    rows = S // GRID_W
    kh = min(NA_ROWS_MAX, rows)
    shp = (B, rows, GRID_W, NA_HEADS, NA_HEAD_DIM)
    q, k, v = q.reshape(shp), k.reshape(shp), v.reshape(shp)
    row_start = jnp.clip(jnp.arange(rows) - kh // 2, 0, rows - kh)
    col = np.arange(GRID_W)
    col_start = np.clip(col - NA_COLS // 2, 0, GRID_W - NA_COLS)
    col_idx = col_start[:, None] + np.arange(NA_COLS)[None, :]
    col_off = col_idx - col[:, None]
    rpb_cols = rpb[:, :, col_off + NA_COLS - 1]
    scale = NA_HEAD_DIM ** -0.5

    def one_row(args):
        q_r, start, r = args
        k_r = lax.dynamic_slice_in_dim(k, start, kh, axis=1)[:, :, col_idx]
        v_r = lax.dynamic_slice_in_dim(v, start, kh, axis=1)[:, :, col_idx]
        bias = jnp.take(rpb_cols, start + jnp.arange(kh) - r + NA_ROWS_MAX - 1, axis=1)
        s = jnp.einsum('bwhd,bawchd->bhwac', q_r, k_r, preferred_element_type=jnp.float32) * scale
        s = s + jnp.transpose(bias, (0, 2, 1, 3)).astype(jnp.float32)[None]
        p = jax.nn.softmax(s.reshape(B, NA_HEADS, GRID_W, kh * NA_COLS), axis=-1).reshape(s.shape)
        return jnp.einsum('bhwac,bawchd->bwhd', p.astype(v.dtype), v_r)

    o = lax.map(one_row, (jnp.moveaxis(q, 1, 0), row_start, jnp.arange(rows)))
    return jnp.moveaxis(o, 0, 1).reshape(B, S, D_OUT)


def setup_inputs(seed: int = 0) -> dict:
    key = jax.random.key(seed)
    ks = jax.random.split(key, 24)
    f32 = jnp.float32
    L = DEPTH

    def nrm(k, shape, scale):
        return jax.random.normal(k, shape, f32) * scale

    def gain(k, shape):
        return 1.0 + 0.05 * jax.random.normal(k, shape, f32)

    return {
        'x': nrm(ks[0], (BATCH, SEQ, D_MODEL), 1.0),
        'norm1_g': gain(ks[1], (L, D_MODEL)),
        'w_in': nrm(ks[2], (L, D_MODEL, IN_COLS), D_MODEL ** -0.5),
        'mla_q_norm_g': gain(ks[3], (L, MLA_Q_RANK)),
        'mla_w_uq': nrm(ks[4], (L, MLA_Q_RANK, MLA_HEADS * (MLA_NOPE_DIM + MLA_ROPE_DIM)), MLA_Q_RANK ** -0.5),
        'mla_kv_norm_g': gain(ks[5], (L, MLA_KV_RANK)),
        'mla_w_uk': nrm(ks[6], (L, MLA_KV_RANK, MLA_HEADS * MLA_NOPE_DIM), MLA_KV_RANK ** -0.5),
        'mla_w_uv': nrm(ks[7], (L, MLA_KV_RANK, MLA_HEADS * MLA_V_DIM), MLA_KV_RANK ** -0.5),
        'diff_lambda_q1': nrm(ks[8], (L, DIFF_QK_DIM), 0.1),
        'diff_lambda_k1': nrm(ks[9], (L, DIFF_QK_DIM), 0.1),
        'diff_lambda_q2': nrm(ks[10], (L, DIFF_QK_DIM), 0.1),
        'diff_lambda_k2': nrm(ks[11], (L, DIFF_QK_DIM), 0.1),
        'diff_subln_g': gain(ks[12], (L, DIFF_V_DIM)),
        'swa_sinks': nrm(ks[13], (L, SWA_Q_HEADS), 0.5),
        'na_rpb': nrm(ks[14], (L, NA_HEADS, 2 * NA_ROWS_MAX - 1, 2 * NA_COLS - 1), 0.1),
        'out_g_mla': gain(ks[15], (L, A_OUT)),
        'out_g_swa': gain(ks[16], (L, C_OUT)),
        'out_g_na': gain(ks[17], (L, D_OUT)),
        'w_out': nrm(ks[18], (L, MIX_WIDTH, D_MODEL), MIX_WIDTH ** -0.5),
        'norm2_g': gain(ks[19], (L, D_MODEL)),
        'w_up': nrm(ks[20], (L, D_MODEL, D_FF), D_MODEL ** -0.5),
        'w_down': nrm(ks[21], (L, D_FF, D_MODEL), D_FF ** -0.5),
        'final_norm_g': gain(ks[22], (D_MODEL,)),
    }


def reference(x, norm1_g, w_in, mla_q_norm_g, mla_w_uq, mla_kv_norm_g, mla_w_uk, mla_w_uv,
              diff_lambda_q1, diff_lambda_k1, diff_lambda_q2, diff_lambda_k2, diff_subln_g,
              swa_sinks, na_rpb, out_g_mla, out_g_swa, out_g_na, w_out, norm2_g, w_up, w_down,
              final_norm_g):
    S = x.shape[1]
    cos32, sin32 = rope_table(S, MLA_ROPE_DIM)
    cos64, sin64 = rope_table(S, SWA_HEAD_DIM)
    splits = _split_points()
    for l in range(DEPTH):
        h = rms_norm(x, norm1_g[l])
        proj = h @ w_in[l]
        (a_cq, a_ckv, a_kr, b_q, b_k, b_v, c_q, c_k, c_v, d_q, d_k, d_v) = jnp.split(proj, splits, axis=-1)
        y_a = mla_mixer(a_cq, a_ckv, a_kr, mla_q_norm_g[l], mla_w_uq[l], mla_kv_norm_g[l],
                        mla_w_uk[l], mla_w_uv[l], cos32, sin32)
        lambda_init = 0.8 - 0.6 * math.exp(-0.3 * l)
        y_b = diff_mixer(b_q, b_k, b_v, diff_lambda_q1[l], diff_lambda_k1[l], diff_lambda_q2[l],
                         diff_lambda_k2[l], diff_subln_g[l], lambda_init, cos32, sin32)
        y_c = swa_mixer(c_q, c_k, c_v, swa_sinks[l], cos64, sin64)
        y_d = na_mixer(d_q, d_k, d_v, na_rpb[l])
        mix = jnp.concatenate([rms_norm(y_a, out_g_mla[l]), y_b,
                               rms_norm(y_c, out_g_swa[l]), rms_norm(y_d, out_g_na[l])], axis=-1)
        x = x + mix @ w_out[l]
        h = rms_norm(x, norm2_g[l])
        x = x + jnp.square(jax.nn.relu(h @ w_up[l])) @ w_down[l]
    return rms_norm(x, final_norm_g)
```

```python
import functools
import math

import jax
import jax.numpy as jnp
import numpy as np
from jax.experimental import pallas as pl
from jax.experimental.pallas import tpu as pltpu

D_MODEL = 1024
DEPTH = 4
GRID_W = 64
ROPE_THETA = 10000.0
EPS = 1e-6
NEG_INF = -1e30

MLA_HEADS = 4
MLA_Q_RANK = 192
MLA_KV_RANK = 128
MLA_NOPE_DIM = 64
MLA_ROPE_DIM = 32
MLA_V_DIM = 64
MLA_QK_DIM = MLA_NOPE_DIM + MLA_ROPE_DIM
DIFF_HEADS = 4
DIFF_QK_DIM = 32
DIFF_V_DIM = 64
SWA_Q_HEADS = 4
SWA_KV_HEADS = 2
SWA_GROUP = SWA_Q_HEADS // SWA_KV_HEADS
SWA_HEAD_DIM = 64
WINDOW = 128
NA_HEADS = 4
NA_HEAD_DIM = 64
NA_ROWS = 8
NA_COLS = 16
D_FF = 4 * D_MODEL
HEAD_V = 64
GROUP_OUT = 256

IN_SPLIT_WIDTHS = (
    MLA_Q_RANK, MLA_KV_RANK, MLA_ROPE_DIM,
    DIFF_HEADS * 2 * DIFF_QK_DIM, DIFF_HEADS * 2 * DIFF_QK_DIM, DIFF_HEADS * DIFF_V_DIM,
    SWA_Q_HEADS * SWA_HEAD_DIM, SWA_KV_HEADS * SWA_HEAD_DIM, SWA_KV_HEADS * SWA_HEAD_DIM,
    NA_HEADS * NA_HEAD_DIM, NA_HEADS * NA_HEAD_DIM, NA_HEADS * NA_HEAD_DIM,
)
IN_COLS = sum(IN_SPLIT_WIDTHS)
IN_OFFSETS = tuple(int(v) for v in np.cumsum((0,) + IN_SPLIT_WIDTHS))

V7X_LANES = 128
V7X_BF16_SUBLANES = 16
V7X_VMEM_LIMIT_BYTES = 56 * 1024 * 1024

LOG2E = math.log2(math.e)
V_AUG = HEAD_V + V7X_BF16_SUBLANES

TOKEN_TILE = 512
DENSE_TQ = 512
DENSE_TK = 1024
LOCAL_BLK = 256
LOCAL_CHUNKS = 3


def _params(n_axes):
    return pltpu.CompilerParams(
        dimension_semantics=("arbitrary",) * n_axes,
        vmem_limit_bytes=V7X_VMEM_LIMIT_BYTES,
    )


def _const_spec(shape):
    nd = len(shape)
    return pl.BlockSpec(shape, lambda *_: (0,) * nd)


def _rms_rows(x, g):
    r = jax.lax.rsqrt(jnp.mean(x * x, axis=0, keepdims=True) + EPS)
    return (x * r) * g


def _rope_rows(x1, x2, c, s):
    return x1 * c - x2 * s, x2 * c + x1 * s


def _ones_rows(tokens):
    row = jax.lax.broadcasted_iota(jnp.int32, (V7X_BF16_SUBLANES, tokens), 0)
    return jnp.where(row == 0, 1.0, 0.0).astype(jnp.bfloat16)


def _prep_kernel(x_ref, g1_ref, w_in_ref, qg_ref, w_uq_ref, kvg_ref, w_uk_ref, w_uv_ref,
                 c32_ref, s32_ref, c64_ref, s64_ref,
                 mq_ref, mk_ref, mv_ref, dq_ref, dk_ref, dv_ref,
                 sq_ref, sk_ref, sv_ref, nq_ref, nk_ref, nv_ref):
    bf16 = jnp.bfloat16
    f32 = jnp.float32
    tokens = x_ref.shape[1]
    h = _rms_rows(x_ref[...], g1_ref[...]).astype(bf16)
    proj = jnp.dot(w_in_ref[...], h, preferred_element_type=f32)
    part = [proj[IN_OFFSETS[i]:IN_OFFSETS[i + 1]] for i in range(len(IN_SPLIT_WIDTHS))]
    a_cq, a_ckv, a_kr, b_q, b_k, b_v, c_q, c_k, c_v, d_q, d_k, d_v = part
    c32, s32 = c32_ref[...], s32_ref[...]
    c64, s64 = c64_ref[...], s64_ref[...]
    ones = _ones_rows(tokens)

    mla_scale = MLA_QK_DIM ** -0.5 * LOG2E
    cqn = _rms_rows(a_cq, qg_ref[...]).astype(bf16)
    q_all = jnp.dot(w_uq_ref[...], cqn, preferred_element_type=f32)
    lat = _rms_rows(a_ckv, kvg_ref[...]).astype(bf16)
    k_nope = jnp.dot(w_uk_ref[...], lat, preferred_element_type=f32)
    v_all = jnp.dot(w_uv_ref[...], lat, preferred_element_type=f32)
    half = MLA_ROPE_DIM // 2
    kp1, kp2 = _rope_rows(a_kr[:half], a_kr[half:], c32, s32)
    for hd in range(MLA_HEADS):
        q = q_all[hd * MLA_QK_DIM:(hd + 1) * MLA_QK_DIM]
        q1, q2 = _rope_rows(q[MLA_NOPE_DIM:MLA_NOPE_DIM + half], q[MLA_NOPE_DIM + half:], c32, s32)
        mq_ref[hd, 0:MLA_NOPE_DIM] = (q[:MLA_NOPE_DIM] * mla_scale).astype(bf16)
        mq_ref[hd, MLA_NOPE_DIM:MLA_NOPE_DIM + half] = (q1 * mla_scale).astype(bf16)
        mq_ref[hd, MLA_NOPE_DIM + half:MLA_QK_DIM] = (q2 * mla_scale).astype(bf16)
        mk_ref[hd, 0:MLA_NOPE_DIM] = k_nope[hd * MLA_NOPE_DIM:(hd + 1) * MLA_NOPE_DIM].astype(bf16)
        mk_ref[hd, MLA_NOPE_DIM:MLA_NOPE_DIM + half] = kp1.astype(bf16)
        mk_ref[hd, MLA_NOPE_DIM + half:MLA_QK_DIM] = kp2.astype(bf16)
        mv_ref[hd, 0:HEAD_V] = v_all[hd * HEAD_V:(hd + 1) * HEAD_V].astype(bf16)
        mv_ref[hd, HEAD_V:V_AUG] = ones

    diff_scale = DIFF_QK_DIM ** -0.5 * LOG2E
    half = DIFF_QK_DIM // 2
    for m in range(2 * DIFF_HEADS):
        lo = m * DIFF_QK_DIM
        q1, q2 = _rope_rows(b_q[lo:lo + half], b_q[lo + half:lo + DIFF_QK_DIM], c32, s32)
        k1, k2 = _rope_rows(b_k[lo:lo + half], b_k[lo + half:lo + DIFF_QK_DIM], c32, s32)
        dq_ref[m, 0:half] = (q1 * diff_scale).astype(bf16)
        dq_ref[m, half:DIFF_QK_DIM] = (q2 * diff_scale).astype(bf16)
        dk_ref[m, 0:half] = k1.astype(bf16)
        dk_ref[m, half:DIFF_QK_DIM] = k2.astype(bf16)
    for hd in range(DIFF_HEADS):
        dv_ref[hd, 0:HEAD_V] = b_v[hd * HEAD_V:(hd + 1) * HEAD_V].astype(bf16)
        dv_ref[hd, HEAD_V:V_AUG] = ones

    swa_scale = SWA_HEAD_DIM ** -0.5 * LOG2E
    half = SWA_HEAD_DIM // 2
    for hd in range(SWA_Q_HEADS):
        lo = hd * SWA_HEAD_DIM
        q1, q2 = _rope_rows(c_q[lo:lo + half], c_q[lo + half:lo + SWA_HEAD_DIM], c64, s64)
        sq_ref[hd, 0:half] = (q1 * swa_scale).astype(bf16)
        sq_ref[hd, half:SWA_HEAD_DIM] = (q2 * swa_scale).astype(bf16)
    for hd in range(SWA_KV_HEADS):
        lo = hd * SWA_HEAD_DIM
        k1, k2 = _rope_rows(c_k[lo:lo + half], c_k[lo + half:lo + SWA_HEAD_DIM], c64, s64)
        sk_ref[hd, 0:half] = k1.astype(bf16)
        sk_ref[hd, half:SWA_HEAD_DIM] = k2.astype(bf16)
        sv_ref[hd, 0:HEAD_V] = c_v[lo:lo + HEAD_V].astype(bf16)
        sv_ref[hd, HEAD_V:V_AUG] = ones

    na_scale = NA_HEAD_DIM ** -0.5 * LOG2E
    for hd in range(NA_HEADS):
        lo = hd * NA_HEAD_DIM
        nq_ref[hd] = (d_q[lo:lo + NA_HEAD_DIM] * na_scale).astype(bf16)
        nk_ref[hd] = d_k[lo:lo + NA_HEAD_DIM].astype(bf16)
        nv_ref[hd, 0:HEAD_V] = d_v[lo:lo + HEAD_V].astype(bf16)
        nv_ref[hd, HEAD_V:V_AUG] = ones


def _prep(xT, g1, w_inT, qg, w_uqT, kvg, w_ukT, w_uvT, c32, s32, c64, s64):
    S = xT.shape[1]
    tt = min(TOKEN_TILE, S)
    bf16 = jnp.bfloat16

    def tok(rows):
        return pl.BlockSpec((rows, tt), lambda i: (0, i))

    def head_out(heads, rows):
        return (jax.ShapeDtypeStruct((heads, rows, S), bf16),
                pl.BlockSpec((heads, rows, tt), lambda i: (0, 0, i)))

    outs = [head_out(MLA_HEADS, MLA_QK_DIM), head_out(MLA_HEADS, MLA_QK_DIM), head_out(MLA_HEADS, V_AUG),
            head_out(2 * DIFF_HEADS, DIFF_QK_DIM), head_out(2 * DIFF_HEADS, DIFF_QK_DIM),
            head_out(DIFF_HEADS, V_AUG),
            head_out(SWA_Q_HEADS, SWA_HEAD_DIM), head_out(SWA_KV_HEADS, SWA_HEAD_DIM),
            head_out(SWA_KV_HEADS, V_AUG),
            head_out(NA_HEADS, NA_HEAD_DIM), head_out(NA_HEADS, NA_HEAD_DIM), head_out(NA_HEADS, V_AUG)]
    return pl.pallas_call(
        _prep_kernel,
        grid=(S // tt,),
        in_specs=[tok(D_MODEL), _const_spec(g1.shape), _const_spec(w_inT.shape), _const_spec(qg.shape),
                  _const_spec(w_uqT.shape), _const_spec(kvg.shape), _const_spec(w_ukT.shape),
                  _const_spec(w_uvT.shape), tok(c32.shape[0]), tok(s32.shape[0]), tok(c64.shape[0]),
                  tok(s64.shape[0])],
        out_specs=[o[1] for o in outs],
        out_shape=[o[0] for o in outs],
        compiler_params=_params(1),
        name="prep",
    )(xT, g1, w_inT, qg, w_uqT, kvg, w_ukT, w_uvT, c32, s32, c64, s64)


def _dense_kernel(q_ref, k_ref, v_ref, o_ref, acc_ref, m_ref):
    f32 = jnp.float32
    nk = k_ref.shape[1]
    q = q_ref[0]
    acc_ref[...] = jnp.zeros_like(acc_ref)
    m_ref[...] = jnp.full_like(m_ref, NEG_INF)

    def body(kb, carry):
        s = jnp.dot(k_ref[0, kb], q, preferred_element_type=f32)
        m_old = m_ref[...]
        m_new = jnp.maximum(m_old, jnp.max(s, axis=0, keepdims=True))
        p = jnp.exp2(s - m_new).astype(jnp.bfloat16)
        alpha = jnp.exp2(m_old - m_new)
        pv = jnp.dot(v_ref[0, kb], p, preferred_element_type=f32)
        acc_ref[...] = alpha * acc_ref[...] + pv
        m_ref[...] = m_new
        return carry

    jax.lax.fori_loop(0, nk, body, 0)
    acc = acc_ref[...]
    o_ref[0] = acc[0:HEAD_V] / acc[HEAD_V:HEAD_V + 1]


def _dense_attention(qT, kT, vT, maps_per_value, name):
    M, d, S = qT.shape
    tq = min(DENSE_TQ, S)
    tk = min(DENSE_TK, S)
    nk = S // tk
    k = jnp.swapaxes(kT, 1, 2).reshape(M, nk, tk, d)
    v = jnp.swapaxes(vT.reshape(vT.shape[0], V_AUG, nk, tk), 1, 2)
    return pl.pallas_call(
        _dense_kernel,
        grid=(M, S // tq),
        in_specs=[pl.BlockSpec((1, d, tq), lambda m, i: (m, 0, i)),
                  pl.BlockSpec((1, nk, tk, d), lambda m, i: (m, 0, 0, 0)),
                  pl.BlockSpec((1, nk, V_AUG, tk), lambda m, i: (m // maps_per_value, 0, 0, 0))],
        out_specs=pl.BlockSpec((1, HEAD_V, tq), lambda m, i: (m, 0, i)),
        out_shape=jax.ShapeDtypeStruct((M, HEAD_V, S), jnp.float32),
        scratch_shapes=[pltpu.VMEM((V_AUG, tq), jnp.float32), pltpu.VMEM((1, tq), jnp.float32)],
        compiler_params=_params(2),
        name=name,
    )(qT, k, v)


def _local_kernel(sink_ref, q_ref, k0_ref, k1_ref, k2_ref, v0_ref, v1_ref, v2_ref, b_ref, o_ref,
                  *, has_sink):
    f32 = jnp.float32
    q = q_ref[0]
    blk = q.shape[1]
    s = [jnp.dot(k[0], q, preferred_element_type=f32) + b_ref[0, 0, c * blk:(c + 1) * blk]
         for c, k in enumerate((k0_ref, k1_ref, k2_ref))]
    m = jnp.max(jnp.maximum(jnp.maximum(s[0], s[1]), s[2]), axis=0, keepdims=True)
    if has_sink:
        sink = sink_ref[pl.program_id(0)]
        m = jnp.maximum(m, sink)
    pv = None
    for sc, v in zip(s, (v0_ref, v1_ref, v2_ref)):
        p = jnp.exp2(sc - m).astype(jnp.bfloat16)
        t = jnp.dot(v[0], p, preferred_element_type=f32)
        pv = t if pv is None else pv + t
    l = pv[HEAD_V:HEAD_V + 1]
    if has_sink:
        l = l + jnp.exp2(sink - m)
    o_ref[0] = pv[0:HEAD_V] / l


def _local_attention(qT, kT, vT, bias, sinks, q_per_kv, name):
    Hq, d, S = qT.shape
    blk = LOCAL_BLK
    nb = S // blk
    assert nb >= LOCAL_CHUNKS
    Hb = bias.shape[1]
    k = jnp.swapaxes(kT, 1, 2)
    has_sink = sinks is not None
    if not has_sink:
        sinks = jnp.zeros((Hq,), jnp.float32)

    def base(i):
        return jnp.clip(i - 1, 0, nb - LOCAL_CHUNKS)

    def case(i):
        return jnp.where(i == 0, 0, jnp.where(i == nb - 1, 2, 1))

    def kspec(c):
        return pl.BlockSpec((1, blk, d), lambda h, i: (h // q_per_kv, base(i) + c, 0))

    def vspec(c):
        return pl.BlockSpec((1, V_AUG, blk), lambda h, i: (h // q_per_kv, 0, base(i) + c))

    return pl.pallas_call(
        functools.partial(_local_kernel, has_sink=has_sink),
        grid=(Hq, nb),
        in_specs=[pl.BlockSpec(memory_space=pltpu.SMEM),
                  pl.BlockSpec((1, d, blk), lambda h, i: (h, 0, i)),
                  kspec(0), kspec(1), kspec(2), vspec(0), vspec(1), vspec(2),
                  pl.BlockSpec((1, 1, LOCAL_CHUNKS * blk, blk),
                               lambda h, i: (case(i), h % Hb, 0, 0))],
        out_specs=pl.BlockSpec((1, HEAD_V, blk), lambda h, i: (h, 0, i)),
        out_shape=jax.ShapeDtypeStruct((Hq, HEAD_V, S), jnp.float32),
        compiler_params=_params(2),
        name=name,
    )(sinks, qT, k, k, k, vT, vT, vT, bias)


def _swa_bias(nb):
    blk = LOCAL_BLK
    kk = np.arange(LOCAL_CHUNKS * blk)[:, None]
    qq = np.arange(blk)[None, :]
    out = []
    for q_off in (0, blk, 2 * blk):
        valid = np.abs(kk - (q_off + qq)) <= WINDOW
        out.append(np.where(valid, 0.0, NEG_INF))
    return jnp.asarray(np.stack(out)[:, None], jnp.float32)


def _na_bias(rpb, S):
    blk = LOCAL_BLK
    rows = S // GRID_W
    rpq = blk // GRID_W
    nb = S // blk
    kk = np.arange(LOCAL_CHUNKS * blk)[:, None]
    qq = np.arange(blk)[None, :]
    kc, w = kk % GRID_W, qq % GRID_W
    cs = np.clip(w - NA_COLS // 2, 0, GRID_W - NA_COLS)
    col_ok = (kc >= cs) & (kc < cs + NA_COLS)
    col_idx = np.broadcast_to(kc - w + NA_COLS - 1, (LOCAL_CHUNKS * blk, blk))
    tables = []
    for i in (0, 1, nb - 1):
        base_row = rpq * int(np.clip(i - 1, 0, nb - LOCAL_CHUNKS))
        r = rpq * i + qq // GRID_W
        kr = base_row + kk // GRID_W
        rs = np.clip(r - NA_ROWS // 2, 0, rows - NA_ROWS)
        valid = col_ok & (kr >= rs) & (kr < rs + NA_ROWS)
        row_idx = np.broadcast_to(kr - r + NA_ROWS - 1, valid.shape)
        row_idx = np.where(valid, row_idx, 0)
        cidx = np.where(valid, col_idx, 0)
        vals = rpb[:, row_idx, cidx] * LOG2E
        tables.append(jnp.where(jnp.asarray(valid)[None], vals, NEG_INF))
    return jnp.stack(tables).astype(jnp.float32)


def _post_kernel(x_ref, oa_ref, ob_ref, oc_ref, od_ref, ga_ref, gc_ref, gd_ref, gsub_ref,
                 lq1_ref, lk1_ref, lq2_ref, lk2_ref, linit_ref, w_out_ref, g2_ref, w_up_ref,
                 w_down_ref, gf_ref, y_ref, *, final):
    bf16 = jnp.bfloat16
    f32 = jnp.float32
    tokens = x_ref.shape[1]

    def heads(ref):
        return ref[...].reshape(ref.shape[0] * ref.shape[1], tokens)

    ya = _rms_rows(heads(oa_ref), ga_ref[...])
    yc = _rms_rows(heads(oc_ref), gc_ref[...])
    yd = _rms_rows(heads(od_ref), gd_ref[...])
    lam_init = linit_ref[...]
    lam = (jnp.exp(jnp.sum(lq1_ref[...] * lk1_ref[...], axis=1, keepdims=True))
           - jnp.exp(jnp.sum(lq2_ref[...] * lk2_ref[...], axis=1, keepdims=True)) + lam_init)
    yb = []
    for hd in range(DIFF_HEADS):
        w = ob_ref[2 * hd] - lam * ob_ref[2 * hd + 1]
        yb.append(_rms_rows(w, gsub_ref[...]) * (1.0 - lam_init))
    mix = jnp.concatenate([ya] + yb + [yc, yd], axis=0).astype(bf16)
    x1 = x_ref[...] + jnp.dot(w_out_ref[...], mix, preferred_element_type=f32)
    h2 = _rms_rows(x1, g2_ref[...]).astype(bf16)
    u = jnp.dot(w_up_ref[...], h2, preferred_element_type=f32)
    a = jnp.square(jnp.maximum(u, 0.0)).astype(bf16)
    x2 = x1 + jnp.dot(w_down_ref[...], a, preferred_element_type=f32)
    if final:
        x2 = _rms_rows(x2, gf_ref[...])
    y_ref[...] = x2


def _post(xT, oa, ob, oc, od, ga, gc, gd, gsub, lq1, lk1, lq2, lk2, linit, w_outT, g2, w_upT,
          w_downT, gf, final):
    S = xT.shape[1]
    tt = min(TOKEN_TILE, S)

    def tok(rows):
        return pl.BlockSpec((rows, tt), lambda i: (0, i))

    def heads(arr):
        return pl.BlockSpec((arr.shape[0], arr.shape[1], tt), lambda i: (0, 0, i))

    def resident(arr):
        nd = arr.ndim
        return pl.BlockSpec(arr.shape, lambda *_: (0,) * nd, pipeline_mode=pl.Buffered(1))

    small = [ga, gc, gd, gsub, lq1, lk1, lq2, lk2, linit]
    return pl.pallas_call(
        functools.partial(_post_kernel, final=final),
        grid=(S // tt,),
        in_specs=[tok(D_MODEL), heads(oa), heads(ob), heads(oc), heads(od)]
                 + [_const_spec(a.shape) for a in small]
                 + [resident(w_outT), _const_spec(g2.shape), resident(w_upT), resident(w_downT),
                    _const_spec(gf.shape)],
        out_specs=tok(D_MODEL),
        out_shape=jax.ShapeDtypeStruct((D_MODEL, S), jnp.float32),
        compiler_params=_params(1),
        name="post_final" if final else "post",
    )(xT, oa, ob, oc, od, *small, w_outT, g2, w_upT, w_downT, gf)


def _rope_tables_T(S, dim):
    inv = 1.0 / (ROPE_THETA ** (jnp.arange(0, dim, 2, dtype=jnp.float32) / dim))
    ang = inv[:, None] * jnp.arange(S, dtype=jnp.float32)[None, :]
    return jnp.cos(ang), jnp.sin(ang)


def _col(v):
    return v.astype(jnp.float32)[:, None]


def _wT(w):
    return jnp.swapaxes(w, -1, -2).astype(jnp.bfloat16)


@jax.jit
def _forward(x, norm1_g, w_in, mla_q_norm_g, mla_w_uq, mla_kv_norm_g, mla_w_uk, mla_w_uv,
             diff_lambda_q1, diff_lambda_k1, diff_lambda_q2, diff_lambda_k2, diff_subln_g,
             swa_sinks, na_rpb, out_g_mla, out_g_swa, out_g_na, w_out, norm2_g, w_up, w_down,
             final_norm_g):
    B, S, D = x.shape
    assert B == 1 and D == D_MODEL and S % TOKEN_TILE == 0 and S % GRID_W == 0
    c32, s32 = _rope_tables_T(S, MLA_ROPE_DIM)
    c64, s64 = _rope_tables_T(S, SWA_HEAD_DIM)
    swa_bias = _swa_bias(S // LOCAL_BLK)
    xT = x[0].T
    gf = _col(final_norm_g)
    for l in range(DEPTH):
        prep = _prep(xT, _col(norm1_g[l]), _wT(w_in[l]), _col(mla_q_norm_g[l]), _wT(mla_w_uq[l]),
                     _col(mla_kv_norm_g[l]), _wT(mla_w_uk[l]), _wT(mla_w_uv[l]), c32, s32, c64, s64)
        mq, mk, mv, dq, dk, dv, sq, sk, sv, nq, nk, nv = prep
        oa = _dense_attention(mq, mk, mv, 1, "dense_mla")
        ob = _dense_attention(dq, dk, dv, 2, "dense_diff")
        oc = _local_attention(sq, sk, sv, swa_bias, swa_sinks[l].astype(jnp.float32) * LOG2E,
                              SWA_GROUP, "local_swa")
        od = _local_attention(nq, nk, nv, _na_bias(na_rpb[l].astype(jnp.float32), S), None, 1,
                              "local_na")
        lam_init = 0.8 - 0.6 * math.exp(-0.3 * l)
        row = lambda v: v.astype(jnp.float32)[None, :]
        xT = _post(xT, oa, ob, oc, od, _col(out_g_mla[l]), _col(out_g_swa[l]), _col(out_g_na[l]),
                   _col(diff_subln_g[l]), row(diff_lambda_q1[l]), row(diff_lambda_k1[l]),
                   row(diff_lambda_q2[l]), row(diff_lambda_k2[l]),
                   jnp.full((1, 1), lam_init, jnp.float32), _wT(w_out[l]), _col(norm2_g[l]),
                   _wT(w_up[l]), _wT(w_down[l]), gf, final=(l == DEPTH - 1))
    return xT.T[None]


def kernel(x, norm1_g, w_in, mla_q_norm_g, mla_w_uq, mla_kv_norm_g, mla_w_uk, mla_w_uv,
           diff_lambda_q1, diff_lambda_k1, diff_lambda_q2, diff_lambda_k2, diff_subln_g,
           swa_sinks, na_rpb, out_g_mla, out_g_swa, out_g_na, w_out, norm2_g, w_up, w_down,
           final_norm_g):
    return _forward(x, norm1_g, w_in, mla_q_norm_g, mla_w_uq, mla_kv_norm_g, mla_w_uk, mla_w_uv,
                    diff_lambda_q1, diff_lambda_k1, diff_lambda_q2, diff_lambda_k2, diff_subln_g,
                    swa_sinks, na_rpb, out_g_mla, out_g_swa, out_g_na, w_out, norm2_g, w_up,
                    w_down, final_norm_g)
```

```python
import functools
import math

import jax
import jax.numpy as jnp
import numpy as np
from jax.experimental import pallas as pl
from jax.experimental.pallas import tpu as pltpu

D_MODEL = 1024
DEPTH = 4
GRID_W = 64
ROPE_THETA = 10000.0
EPS = 1e-6
NEG_INF = -1e30

MLA_HEADS = 4
MLA_Q_RANK = 192
MLA_KV_RANK = 128
MLA_NOPE_DIM = 64
MLA_ROPE_DIM = 32
MLA_V_DIM = 64
MLA_QK_DIM = MLA_NOPE_DIM + MLA_ROPE_DIM
DIFF_HEADS = 4
DIFF_QK_DIM = 32
DIFF_V_DIM = 64
SWA_Q_HEADS = 4
SWA_KV_HEADS = 2
SWA_GROUP = SWA_Q_HEADS // SWA_KV_HEADS
SWA_HEAD_DIM = 64
WINDOW = 128
NA_HEADS = 4
NA_HEAD_DIM = 64
NA_ROWS = 8
NA_COLS = 16
D_FF = 4 * D_MODEL
HEAD_V = 64
GROUP_OUT = 256

IN_SPLIT_WIDTHS = (
    MLA_Q_RANK, MLA_KV_RANK, MLA_ROPE_DIM,
    DIFF_HEADS * 2 * DIFF_QK_DIM, DIFF_HEADS * 2 * DIFF_QK_DIM, DIFF_HEADS * DIFF_V_DIM,
    SWA_Q_HEADS * SWA_HEAD_DIM, SWA_KV_HEADS * SWA_HEAD_DIM, SWA_KV_HEADS * SWA_HEAD_DIM,
    NA_HEADS * NA_HEAD_DIM, NA_HEADS * NA_HEAD_DIM, NA_HEADS * NA_HEAD_DIM,
)
IN_COLS = sum(IN_SPLIT_WIDTHS)
IN_OFFSETS = tuple(int(v) for v in np.cumsum((0,) + IN_SPLIT_WIDTHS))

V7X_LANES = 128
V7X_BF16_SUBLANES = 16
V7X_VMEM_LIMIT_BYTES = 56 * 1024 * 1024

LOG2E = math.log2(math.e)
V_AUG = HEAD_V + V7X_BF16_SUBLANES

TOKEN_TILE = 512
DENSE_TQ = 512
DENSE_TK = 1024
LOCAL_BLK = 256
LOCAL_CHUNKS = 3


def _params(n_axes):
    return pltpu.CompilerParams(
        dimension_semantics=("arbitrary",) * n_axes,
        vmem_limit_bytes=V7X_VMEM_LIMIT_BYTES,
    )


def _const_spec(shape):
    nd = len(shape)
    return pl.BlockSpec(shape, lambda *_: (0,) * nd)


def _rms_rows(x, g):
    r = jax.lax.rsqrt(jnp.mean(x * x, axis=0, keepdims=True) + EPS)
    return (x * r) * g


def _rope_rows(x1, x2, c, s):
    return x1 * c - x2 * s, x2 * c + x1 * s


def _ones_rows(tokens):
    row = jax.lax.broadcasted_iota(jnp.int32, (V7X_BF16_SUBLANES, tokens), 0)
    return jnp.where(row == 0, 1.0, 0.0).astype(jnp.bfloat16)


def _prep_kernel(x_ref, g1_ref, w_in_ref, qg_ref, w_uq_ref, kvg_ref, w_uk_ref, w_uv_ref,
                 c32_ref, s32_ref, c64_ref, s64_ref,
                 mq_ref, mk_ref, mv_ref, dq_ref, dk_ref, dv_ref,
                 sq_ref, sk_ref, sv_ref, nq_ref, nk_ref, nv_ref):
    bf16 = jnp.bfloat16
    f32 = jnp.float32
    tokens = x_ref.shape[1]
    h = _rms_rows(x_ref[...], g1_ref[...]).astype(bf16)
    proj = jnp.dot(w_in_ref[...], h, preferred_element_type=f32)
    part = [proj[IN_OFFSETS[i]:IN_OFFSETS[i + 1]] for i in range(len(IN_SPLIT_WIDTHS))]
    a_cq, a_ckv, a_kr, b_q, b_k, b_v, c_q, c_k, c_v, d_q, d_k, d_v = part
    c32, s32 = c32_ref[...], s32_ref[...]
    c64, s64 = c64_ref[...], s64_ref[...]
    ones = _ones_rows(tokens)

    mla_scale = MLA_QK_DIM ** -0.5 * LOG2E
    cqn = _rms_rows(a_cq, qg_ref[...]).astype(bf16)
    q_all = jnp.dot(w_uq_ref[...], cqn, preferred_element_type=f32)
    lat = _rms_rows(a_ckv, kvg_ref[...]).astype(bf16)
    k_nope = jnp.dot(w_uk_ref[...], lat, preferred_element_type=f32)
    v_all = jnp.dot(w_uv_ref[...], lat, preferred_element_type=f32)
    half = MLA_ROPE_DIM // 2
    kp1, kp2 = _rope_rows(a_kr[:half], a_kr[half:], c32, s32)
    for hd in range(MLA_HEADS):
        q = q_all[hd * MLA_QK_DIM:(hd + 1) * MLA_QK_DIM]
        q1, q2 = _rope_rows(q[MLA_NOPE_DIM:MLA_NOPE_DIM + half], q[MLA_NOPE_DIM + half:], c32, s32)
        mq_ref[hd, 0:MLA_NOPE_DIM] = (q[:MLA_NOPE_DIM] * mla_scale).astype(bf16)
        mq_ref[hd, MLA_NOPE_DIM:MLA_NOPE_DIM + half] = (q1 * mla_scale).astype(bf16)
        mq_ref[hd, MLA_NOPE_DIM + half:MLA_QK_DIM] = (q2 * mla_scale).astype(bf16)
        mk_ref[hd, 0:MLA_NOPE_DIM] = k_nope[hd * MLA_NOPE_DIM:(hd + 1) * MLA_NOPE_DIM].astype(bf16)
        mk_ref[hd, MLA_NOPE_DIM:MLA_NOPE_DIM + half] = kp1.astype(bf16)
        mk_ref[hd, MLA_NOPE_DIM + half:MLA_QK_DIM] = kp2.astype(bf16)
        mv_ref[hd, 0:HEAD_V] = v_all[hd * HEAD_V:(hd + 1) * HEAD_V].astype(bf16)
        mv_ref[hd, HEAD_V:V_AUG] = ones

    diff_scale = DIFF_QK_DIM ** -0.5 * LOG2E
    half = DIFF_QK_DIM // 2
    for m in range(2 * DIFF_HEADS):
        lo = m * DIFF_QK_DIM
        q1, q2 = _rope_rows(b_q[lo:lo + half], b_q[lo + half:lo + DIFF_QK_DIM], c32, s32)
        k1, k2 = _rope_rows(b_k[lo:lo + half], b_k[lo + half:lo + DIFF_QK_DIM], c32, s32)
        dq_ref[m, 0:half] = (q1 * diff_scale).astype(bf16)
        dq_ref[m, half:DIFF_QK_DIM] = (q2 * diff_scale).astype(bf16)
        dk_ref[m, 0:half] = k1.astype(bf16)
        dk_ref[m, half:DIFF_QK_DIM] = k2.astype(bf16)
    for hd in range(DIFF_HEADS):
        dv_ref[hd, 0:HEAD_V] = b_v[hd * HEAD_V:(hd + 1) * HEAD_V].astype(bf16)
        dv_ref[hd, HEAD_V:V_AUG] = ones

    swa_scale = SWA_HEAD_DIM ** -0.5 * LOG2E
    half = SWA_HEAD_DIM // 2
    for hd in range(SWA_Q_HEADS):
        lo = hd * SWA_HEAD_DIM
        q1, q2 = _rope_rows(c_q[lo:lo + half], c_q[lo + half:lo + SWA_HEAD_DIM], c64, s64)
        sq_ref[hd, 0:half] = (q1 * swa_scale).astype(bf16)
        sq_ref[hd, half:SWA_HEAD_DIM] = (q2 * swa_scale).astype(bf16)
    for hd in range(SWA_KV_HEADS):
        lo = hd * SWA_HEAD_DIM
        k1, k2 = _rope_rows(c_k[lo:lo + half], c_k[lo + half:lo + SWA_HEAD_DIM], c64, s64)
        sk_ref[hd, 0:half] = k1.astype(bf16)
        sk_ref[hd, half:SWA_HEAD_DIM] = k2.astype(bf16)
        sv_ref[hd, 0:HEAD_V] = c_v[lo:lo + HEAD_V].astype(bf16)
        sv_ref[hd, HEAD_V:V_AUG] = ones

    na_scale = NA_HEAD_DIM ** -0.5 * LOG2E
    for hd in range(NA_HEADS):
        lo = hd * NA_HEAD_DIM
        nq_ref[hd] = (d_q[lo:lo + NA_HEAD_DIM] * na_scale).astype(bf16)
        nk_ref[hd] = d_k[lo:lo + NA_HEAD_DIM].astype(bf16)
        nv_ref[hd, 0:HEAD_V] = d_v[lo:lo + HEAD_V].astype(bf16)
        nv_ref[hd, HEAD_V:V_AUG] = ones


def _prep(xT, g1, w_inT, qg, w_uqT, kvg, w_ukT, w_uvT, c32, s32, c64, s64):
    S = xT.shape[1]
    tt = min(TOKEN_TILE, S)
    bf16 = jnp.bfloat16

    def tok(rows):
        return pl.BlockSpec((rows, tt), lambda i: (0, i))

    def head_out(heads, rows):
        return (jax.ShapeDtypeStruct((heads, rows, S), bf16),
                pl.BlockSpec((heads, rows, tt), lambda i: (0, 0, i)))

    outs = [head_out(MLA_HEADS, MLA_QK_DIM), head_out(MLA_HEADS, MLA_QK_DIM), head_out(MLA_HEADS, V_AUG),
            head_out(2 * DIFF_HEADS, DIFF_QK_DIM), head_out(2 * DIFF_HEADS, DIFF_QK_DIM),
            head_out(DIFF_HEADS, V_AUG),
            head_out(SWA_Q_HEADS, SWA_HEAD_DIM), head_out(SWA_KV_HEADS, SWA_HEAD_DIM),
            head_out(SWA_KV_HEADS, V_AUG),
            head_out(NA_HEADS, NA_HEAD_DIM), head_out(NA_HEADS, NA_HEAD_DIM), head_out(NA_HEADS, V_AUG)]
    return pl.pallas_call(
        _prep_kernel,
        grid=(S // tt,),
        in_specs=[tok(D_MODEL), _const_spec(g1.shape), _const_spec(w_inT.shape), _const_spec(qg.shape),
                  _const_spec(w_uqT.shape), _const_spec(kvg.shape), _const_spec(w_ukT.shape),
                  _const_spec(w_uvT.shape), tok(c32.shape[0]), tok(s32.shape[0]), tok(c64.shape[0]),
                  tok(s64.shape[0])],
        out_specs=[o[1] for o in outs],
        out_shape=[o[0] for o in outs],
        compiler_params=_params(1),
        name="prep",
    )(xT, g1, w_inT, qg, w_uqT, kvg, w_ukT, w_uvT, c32, s32, c64, s64)


def _dense_kernel(q_ref, k_ref, v_ref, o_ref, acc_ref, m_ref, s0_ref, s1_ref):
    f32 = jnp.float32
    nk = k_ref.shape[1]
    q = q_ref[0]
    acc_ref[...] = jnp.zeros_like(acc_ref)
    m_ref[...] = jnp.full_like(m_ref, NEG_INF)

    def scores(kb, s_ref):
        s_ref[...] = jnp.dot(k_ref[0, kb], q, preferred_element_type=f32)

    def softmax_pv(kb, s_ref):
        s = s_ref[...]
        m_old = m_ref[...]
        m_new = jnp.maximum(m_old, jnp.max(s, axis=0, keepdims=True))
        p = jnp.exp2(s - m_new).astype(jnp.bfloat16)
        alpha = jnp.exp2(m_old - m_new)
        pv = jnp.dot(v_ref[0, kb], p, preferred_element_type=f32)
        acc_ref[...] = alpha * acc_ref[...] + pv
        m_ref[...] = m_new

    def pair(j, carry):
        kb = 2 * j
        scores(kb + 1, s1_ref)
        softmax_pv(kb, s0_ref)
        scores(kb + 2, s0_ref)
        softmax_pv(kb + 1, s1_ref)
        return carry

    scores(0, s0_ref)
    jax.lax.fori_loop(0, nk // 2 - 1, pair, 0)
    scores(nk - 1, s1_ref)
    softmax_pv(nk - 2, s0_ref)
    softmax_pv(nk - 1, s1_ref)
    acc = acc_ref[...]
    o_ref[0] = acc[0:HEAD_V] / acc[HEAD_V:HEAD_V + 1]


def _dense_attention(qT, kT, vT, maps_per_value, name):
    M, d, S = qT.shape
    tq = min(DENSE_TQ, S)
    tk = min(DENSE_TK, S)
    nk = S // tk
    assert nk >= 2 and nk % 2 == 0
    k = jnp.swapaxes(kT, 1, 2).reshape(M, nk, tk, d)
    v = jnp.swapaxes(vT.reshape(vT.shape[0], V_AUG, nk, tk), 1, 2)
    return pl.pallas_call(
        _dense_kernel,
        grid=(M, S // tq),
        in_specs=[pl.BlockSpec((1, d, tq), lambda m, i: (m, 0, i)),
                  pl.BlockSpec((1, nk, tk, d), lambda m, i: (m, 0, 0, 0)),
                  pl.BlockSpec((1, nk, V_AUG, tk), lambda m, i: (m // maps_per_value, 0, 0, 0))],
        out_specs=pl.BlockSpec((1, HEAD_V, tq), lambda m, i: (m, 0, i)),
        out_shape=jax.ShapeDtypeStruct((M, HEAD_V, S), jnp.float32),
        scratch_shapes=[pltpu.VMEM((V_AUG, tq), jnp.float32), pltpu.VMEM((1, tq), jnp.float32),
                        pltpu.VMEM((tk, tq), jnp.float32), pltpu.VMEM((tk, tq), jnp.float32)],
        compiler_params=_params(2),
        name=name,
    )(qT, k, v)


def _local_kernel(sink_ref, q_ref, k0_ref, k1_ref, k2_ref, v0_ref, v1_ref, v2_ref, b_ref, o_ref,
                  *, has_sink):
    f32 = jnp.float32
    q = q_ref[0]
    blk = q.shape[1]
    s = [jnp.dot(k[0], q, preferred_element_type=f32) + b_ref[0, 0, c * blk:(c + 1) * blk]
         for c, k in enumerate((k0_ref, k1_ref, k2_ref))]
    m = jnp.max(jnp.maximum(jnp.maximum(s[0], s[1]), s[2]), axis=0, keepdims=True)
    if has_sink:
        sink = sink_ref[pl.program_id(0)]
        m = jnp.maximum(m, sink)
    pv = None
    for sc, v in zip(s, (v0_ref, v1_ref, v2_ref)):
        p = jnp.exp2(sc - m).astype(jnp.bfloat16)
        t = jnp.dot(v[0], p, preferred_element_type=f32)
        pv = t if pv is None else pv + t
    l = pv[HEAD_V:HEAD_V + 1]
    if has_sink:
        l = l + jnp.exp2(sink - m)
    o_ref[0] = pv[0:HEAD_V] / l


def _local_attention(qT, kT, vT, bias, sinks, q_per_kv, name):
    Hq, d, S = qT.shape
    blk = LOCAL_BLK
    nb = S // blk
    assert nb >= LOCAL_CHUNKS
    Hb = bias.shape[1]
    k = jnp.swapaxes(kT, 1, 2)
    has_sink = sinks is not None
    if not has_sink:
        sinks = jnp.zeros((Hq,), jnp.float32)

    def base(i):
        return jnp.clip(i - 1, 0, nb - LOCAL_CHUNKS)

    def case(i):
        return jnp.where(i == 0, 0, jnp.where(i == nb - 1, 2, 1))

    def kspec(c):
        return pl.BlockSpec((1, blk, d), lambda h, i: (h // q_per_kv, base(i) + c, 0))

    def vspec(c):
        return pl.BlockSpec((1, V_AUG, blk), lambda h, i: (h // q_per_kv, 0, base(i) + c))

    return pl.pallas_call(
        functools.partial(_local_kernel, has_sink=has_sink),
        grid=(Hq, nb),
        in_specs=[pl.BlockSpec(memory_space=pltpu.SMEM),
                  pl.BlockSpec((1, d, blk), lambda h, i: (h, 0, i)),
                  kspec(0), kspec(1), kspec(2), vspec(0), vspec(1), vspec(2),
                  pl.BlockSpec((1, 1, LOCAL_CHUNKS * blk, blk),
                               lambda h, i: (case(i), h % Hb, 0, 0))],
        out_specs=pl.BlockSpec((1, HEAD_V, blk), lambda h, i: (h, 0, i)),
        out_shape=jax.ShapeDtypeStruct((Hq, HEAD_V, S), jnp.float32),
        compiler_params=_params(2),
        name=name,
    )(sinks, qT, k, k, k, vT, vT, vT, bias)


def _swa_bias(nb):
    blk = LOCAL_BLK
    kk = np.arange(LOCAL_CHUNKS * blk)[:, None]
    qq = np.arange(blk)[None, :]
    out = []
    for q_off in (0, blk, 2 * blk):
        valid = np.abs(kk - (q_off + qq)) <= WINDOW
        out.append(np.where(valid, 0.0, NEG_INF))
    return jnp.asarray(np.stack(out)[:, None], jnp.float32)


def _na_bias(rpb, S):
    blk = LOCAL_BLK
    rows = S // GRID_W
    rpq = blk // GRID_W
    kpq = LOCAL_CHUNKS * rpq
    nb = S // blk
    n_off = 2 * NA_ROWS - 1
    kc = np.arange(GRID_W)[:, None]
    w = np.arange(GRID_W)[None, :]
    cs = np.clip(w - NA_COLS // 2, 0, GRID_W - NA_COLS)
    col_ok = (kc >= cs) & (kc < cs + NA_COLS)
    onehot = (kc - w + NA_COLS - 1)[None] == np.arange(2 * NA_COLS - 1)[:, None, None]
    onehot = jnp.asarray(onehot.reshape(2 * NA_COLS - 1, GRID_W * GRID_W), jnp.float32)
    block_of = np.full((3, kpq, rpq), n_off, np.int32)
    for c, i in enumerate((0, 1, nb - 1)):
        base_row = rpq * int(np.clip(i - 1, 0, nb - LOCAL_CHUNKS))
        for ki in range(kpq):
            for qj in range(rpq):
                r, kr = rpq * i + qj, base_row + ki
                rs = int(np.clip(r - NA_ROWS // 2, 0, rows - NA_ROWS))
                if rs <= kr < rs + NA_ROWS:
                    block_of[c, ki, qj] = kr - r + NA_ROWS - 1
    L, H = rpb.shape[:2]
    toe = jnp.einsum("lhab,bn->lhan", rpb, onehot, precision=jax.lax.Precision.HIGHEST)
    toe = toe.reshape(L, H, n_off, GRID_W, GRID_W) * LOG2E
    toe = jnp.where(jnp.asarray(col_ok), toe, NEG_INF)
    toe = jnp.concatenate([toe, jnp.full((L, H, 1, GRID_W, GRID_W), NEG_INF, jnp.float32)], axis=2)
    t = jnp.take(toe, jnp.asarray(block_of.reshape(-1)), axis=2)
    t = t.reshape(L, H, 3, kpq, rpq, GRID_W, GRID_W)
    t = jnp.transpose(t, (0, 2, 1, 3, 5, 4, 6))
    return t.reshape(L, 3, H, kpq * GRID_W, rpq * GRID_W)


def _post_kernel(x_ref, oa_ref, ob_ref, oc_ref, od_ref, ga_ref, gc_ref, gd_ref, gsub_ref,
                 lq1_ref, lk1_ref, lq2_ref, lk2_ref, linit_ref, w_out_ref, g2_ref, w_up_ref,
                 w_down_ref, gf_ref, y_ref, *, final):
    bf16 = jnp.bfloat16
    f32 = jnp.float32
    tokens = x_ref.shape[1]

    def heads(ref):
        return ref[...].reshape(ref.shape[0] * ref.shape[1], tokens)

    ya = _rms_rows(heads(oa_ref), ga_ref[...])
    yc = _rms_rows(heads(oc_ref), gc_ref[...])
    yd = _rms_rows(heads(od_ref), gd_ref[...])
    lam_init = linit_ref[...]
    lam = (jnp.exp(jnp.sum(lq1_ref[...] * lk1_ref[...], axis=1, keepdims=True))
           - jnp.exp(jnp.sum(lq2_ref[...] * lk2_ref[...], axis=1, keepdims=True)) + lam_init)
    yb = []
    for hd in range(DIFF_HEADS):
        w = ob_ref[2 * hd] - lam * ob_ref[2 * hd + 1]
        yb.append(_rms_rows(w, gsub_ref[...]) * (1.0 - lam_init))
    mix = jnp.concatenate([ya] + yb + [yc, yd], axis=0).astype(bf16)
    x1 = x_ref[...] + jnp.dot(w_out_ref[...], mix, preferred_element_type=f32)
    h2 = _rms_rows(x1, g2_ref[...]).astype(bf16)
    u = jnp.dot(w_up_ref[...], h2, preferred_element_type=f32)
    a = jnp.square(jnp.maximum(u, 0.0)).astype(bf16)
    x2 = x1 + jnp.dot(w_down_ref[...], a, preferred_element_type=f32)
    if final:
        x2 = _rms_rows(x2, gf_ref[...])
    y_ref[...] = x2


def _post(xT, oa, ob, oc, od, ga, gc, gd, gsub, lq1, lk1, lq2, lk2, linit, w_outT, g2, w_upT,
          w_downT, gf, final):
    S = xT.shape[1]
    tt = min(TOKEN_TILE, S)

    def tok(rows):
        return pl.BlockSpec((rows, tt), lambda i: (0, i))

    def heads(arr):
        return pl.BlockSpec((arr.shape[0], arr.shape[1], tt), lambda i: (0, 0, i))

    def resident(arr):
        nd = arr.ndim
        return pl.BlockSpec(arr.shape, lambda *_: (0,) * nd, pipeline_mode=pl.Buffered(1))

    small = [ga, gc, gd, gsub, lq1, lk1, lq2, lk2, linit]
    return pl.pallas_call(
        functools.partial(_post_kernel, final=final),
        grid=(S // tt,),
        in_specs=[tok(D_MODEL), heads(oa), heads(ob), heads(oc), heads(od)]
                 + [_const_spec(a.shape) for a in small]
                 + [resident(w_outT), _const_spec(g2.shape), resident(w_upT), resident(w_downT),
                    _const_spec(gf.shape)],
        out_specs=tok(D_MODEL),
        out_shape=jax.ShapeDtypeStruct((D_MODEL, S), jnp.float32),
        compiler_params=_params(1),
        name="post_final" if final else "post",
    )(xT, oa, ob, oc, od, *small, w_outT, g2, w_upT, w_downT, gf)


def _rope_tables_T(S, dim):
    inv = 1.0 / (ROPE_THETA ** (jnp.arange(0, dim, 2, dtype=jnp.float32) / dim))
    ang = inv[:, None] * jnp.arange(S, dtype=jnp.float32)[None, :]
    return jnp.cos(ang), jnp.sin(ang)


def _col(v):
    return v.astype(jnp.float32)[:, None]


def _wT(w):
    return jnp.swapaxes(w, -1, -2).astype(jnp.bfloat16)


@jax.jit
def _forward(x, norm1_g, w_in, mla_q_norm_g, mla_w_uq, mla_kv_norm_g, mla_w_uk, mla_w_uv,
             diff_lambda_q1, diff_lambda_k1, diff_lambda_q2, diff_lambda_k2, diff_subln_g,
             swa_sinks, na_rpb, out_g_mla, out_g_swa, out_g_na, w_out, norm2_g, w_up, w_down,
             final_norm_g):
    B, S, D = x.shape
    assert B == 1 and D == D_MODEL and S % TOKEN_TILE == 0 and S % GRID_W == 0
    c32, s32 = _rope_tables_T(S, MLA_ROPE_DIM)
    c64, s64 = _rope_tables_T(S, SWA_HEAD_DIM)
    swa_bias = _swa_bias(S // LOCAL_BLK)
    na_bias = _na_bias(na_rpb.astype(jnp.float32), S)
    xT = x[0].T
    gf = _col(final_norm_g)
    for l in range(DEPTH):
        prep = _prep(xT, _col(norm1_g[l]), _wT(w_in[l]), _col(mla_q_norm_g[l]), _wT(mla_w_uq[l]),
                     _col(mla_kv_norm_g[l]), _wT(mla_w_uk[l]), _wT(mla_w_uv[l]), c32, s32, c64, s64)
        mq, mk, mv, dq, dk, dv, sq, sk, sv, nq, nk, nv = prep
        oa = _dense_attention(mq, mk, mv, 1, "dense_mla")
        ob = _dense_attention(dq, dk, dv, 2, "dense_diff")
        oc = _local_attention(sq, sk, sv, swa_bias, swa_sinks[l].astype(jnp.float32) * LOG2E,
                              SWA_GROUP, "local_swa")
        od = _local_attention(nq, nk, nv, na_bias[l], None, 1, "local_na")
        lam_init = 0.8 - 0.6 * math.exp(-0.3 * l)
        row = lambda v: v.astype(jnp.float32)[None, :]
        xT = _post(xT, oa, ob, oc, od, _col(out_g_mla[l]), _col(out_g_swa[l]), _col(out_g_na[l]),
                   _col(diff_subln_g[l]), row(diff_lambda_q1[l]), row(diff_lambda_k1[l]),
                   row(diff_lambda_q2[l]), row(diff_lambda_k2[l]),
                   jnp.full((1, 1), lam_init, jnp.float32), _wT(w_out[l]), _col(norm2_g[l]),
                   _wT(w_up[l]), _wT(w_down[l]), gf, final=(l == DEPTH - 1))
    return xT.T[None]


def kernel(x, norm1_g, w_in, mla_q_norm_g, mla_w_uq, mla_kv_norm_g, mla_w_uk, mla_w_uv,
           diff_lambda_q1, diff_lambda_k1, diff_lambda_q2, diff_lambda_k2, diff_subln_g,
           swa_sinks, na_rpb, out_g_mla, out_g_swa, out_g_na, w_out, norm2_g, w_up, w_down,
           final_norm_g):
    return _forward(x, norm1_g, w_in, mla_q_norm_g, mla_w_uq, mla_kv_norm_g, mla_w_uk, mla_w_uv,
                    diff_lambda_q1, diff_lambda_k1, diff_lambda_q2, diff_lambda_k2, diff_subln_g,
                    swa_sinks, na_rpb, out_g_mla, out_g_swa, out_g_na, w_out, norm2_g, w_up,
                    w_down, final_norm_g)
```

```python
import functools
import math

import jax
import jax.numpy as jnp
import numpy as np
from jax.experimental import pallas as pl
from jax.experimental.pallas import tpu as pltpu

D_MODEL = 1024
DEPTH = 4
GRID_W = 64
ROPE_THETA = 10000.0
EPS = 1e-6
NEG_INF = -1e30

MLA_HEADS = 4
MLA_Q_RANK = 192
MLA_KV_RANK = 128
MLA_NOPE_DIM = 64
MLA_ROPE_DIM = 32
MLA_V_DIM = 64
MLA_QK_DIM = MLA_NOPE_DIM + MLA_ROPE_DIM
DIFF_HEADS = 4
DIFF_QK_DIM = 32
DIFF_V_DIM = 64
SWA_Q_HEADS = 4
SWA_KV_HEADS = 2
SWA_GROUP = SWA_Q_HEADS // SWA_KV_HEADS
SWA_HEAD_DIM = 64
WINDOW = 128
NA_HEADS = 4
NA_HEAD_DIM = 64
NA_ROWS = 8
NA_COLS = 16
D_FF = 4 * D_MODEL
HEAD_V = 64
GROUP_OUT = 256

IN_SPLIT_WIDTHS = (
    MLA_Q_RANK, MLA_KV_RANK, MLA_ROPE_DIM,
    DIFF_HEADS * 2 * DIFF_QK_DIM, DIFF_HEADS * 2 * DIFF_QK_DIM, DIFF_HEADS * DIFF_V_DIM,
    SWA_Q_HEADS * SWA_HEAD_DIM, SWA_KV_HEADS * SWA_HEAD_DIM, SWA_KV_HEADS * SWA_HEAD_DIM,
    NA_HEADS * NA_HEAD_DIM, NA_HEADS * NA_HEAD_DIM, NA_HEADS * NA_HEAD_DIM,
)
IN_COLS = sum(IN_SPLIT_WIDTHS)
IN_OFFSETS = tuple(int(v) for v in np.cumsum((0,) + IN_SPLIT_WIDTHS))

V7X_LANES = 128
V7X_BF16_SUBLANES = 16
V7X_VMEM_LIMIT_BYTES = 56 * 1024 * 1024

LOG2E = math.log2(math.e)
V_AUG = HEAD_V + V7X_BF16_SUBLANES

TOKEN_TILE = 512
DENSE_TQ = 512
DENSE_TK = 4096
DENSE_SUB = 256
LOCAL_BLK = 256
LOCAL_CHUNKS = 3


def _params(n_axes):
    return pltpu.CompilerParams(
        dimension_semantics=("arbitrary",) * n_axes,
        vmem_limit_bytes=V7X_VMEM_LIMIT_BYTES,
    )


def _const_spec(shape):
    nd = len(shape)
    return pl.BlockSpec(shape, lambda *_: (0,) * nd)


def _rms_rows(x, g):
    r = jax.lax.rsqrt(jnp.mean(x * x, axis=0, keepdims=True) + EPS)
    return (x * r) * g


def _rope_rows(x1, x2, c, s):
    return x1 * c - x2 * s, x2 * c + x1 * s


def _ones_rows(tokens):
    row = jax.lax.broadcasted_iota(jnp.int32, (V7X_BF16_SUBLANES, tokens), 0)
    return jnp.where(row == 0, 1.0, 0.0).astype(jnp.bfloat16)


def _prep_kernel(x_ref, g1_ref, w_in_ref, qg_ref, w_uq_ref, kvg_ref, w_uk_ref, w_uv_ref,
                 c32_ref, s32_ref, c64_ref, s64_ref,
                 mq_ref, mk_ref, mv_ref, dq_ref, dk_ref, dv_ref,
                 sq_ref, sk_ref, sv_ref, nq_ref, nk_ref, nv_ref):
    bf16 = jnp.bfloat16
    f32 = jnp.float32
    tokens = x_ref.shape[1]
    h = _rms_rows(x_ref[...], g1_ref[...]).astype(bf16)
    proj = jnp.dot(w_in_ref[...], h, preferred_element_type=f32)
    part = [proj[IN_OFFSETS[i]:IN_OFFSETS[i + 1]] for i in range(len(IN_SPLIT_WIDTHS))]
    a_cq, a_ckv, a_kr, b_q, b_k, b_v, c_q, c_k, c_v, d_q, d_k, d_v = part
    c32, s32 = c32_ref[...], s32_ref[...]
    c64, s64 = c64_ref[...], s64_ref[...]
    ones = _ones_rows(tokens)

    mla_scale = MLA_QK_DIM ** -0.5 * LOG2E
    cqn = _rms_rows(a_cq, qg_ref[...]).astype(bf16)
    q_all = jnp.dot(w_uq_ref[...], cqn, preferred_element_type=f32)
    lat = _rms_rows(a_ckv, kvg_ref[...]).astype(bf16)
    k_nope = jnp.dot(w_uk_ref[...], lat, preferred_element_type=f32)
    v_all = jnp.dot(w_uv_ref[...], lat, preferred_element_type=f32)
    half = MLA_ROPE_DIM // 2
    kp1, kp2 = _rope_rows(a_kr[:half], a_kr[half:], c32, s32)
    for hd in range(MLA_HEADS):
        q = q_all[hd * MLA_QK_DIM:(hd + 1) * MLA_QK_DIM]
        q1, q2 = _rope_rows(q[MLA_NOPE_DIM:MLA_NOPE_DIM + half], q[MLA_NOPE_DIM + half:], c32, s32)
        mq_ref[hd, 0:MLA_NOPE_DIM] = (q[:MLA_NOPE_DIM] * mla_scale).astype(bf16)
        mq_ref[hd, MLA_NOPE_DIM:MLA_NOPE_DIM + half] = (q1 * mla_scale).astype(bf16)
        mq_ref[hd, MLA_NOPE_DIM + half:MLA_QK_DIM] = (q2 * mla_scale).astype(bf16)
        mk_ref[hd, 0:MLA_NOPE_DIM] = k_nope[hd * MLA_NOPE_DIM:(hd + 1) * MLA_NOPE_DIM].astype(bf16)
        mk_ref[hd, MLA_NOPE_DIM:MLA_NOPE_DIM + half] = kp1.astype(bf16)
        mk_ref[hd, MLA_NOPE_DIM + half:MLA_QK_DIM] = kp2.astype(bf16)
        mv_ref[hd, 0:HEAD_V] = v_all[hd * HEAD_V:(hd + 1) * HEAD_V].astype(bf16)
        mv_ref[hd, HEAD_V:V_AUG] = ones

    diff_scale = DIFF_QK_DIM ** -0.5 * LOG2E
    half = DIFF_QK_DIM // 2
    for m in range(2 * DIFF_HEADS):
        lo = m * DIFF_QK_DIM
        q1, q2 = _rope_rows(b_q[lo:lo + half], b_q[lo + half:lo + DIFF_QK_DIM], c32, s32)
        k1, k2 = _rope_rows(b_k[lo:lo + half], b_k[lo + half:lo + DIFF_QK_DIM], c32, s32)
        dq_ref[m, 0:half] = (q1 * diff_scale).astype(bf16)
        dq_ref[m, half:DIFF_QK_DIM] = (q2 * diff_scale).astype(bf16)
        dk_ref[m, 0:half] = k1.astype(bf16)
        dk_ref[m, half:DIFF_QK_DIM] = k2.astype(bf16)
    for hd in range(DIFF_HEADS):
        dv_ref[hd, 0:HEAD_V] = b_v[hd * HEAD_V:(hd + 1) * HEAD_V].astype(bf16)
        dv_ref[hd, HEAD_V:V_AUG] = ones

    swa_scale = SWA_HEAD_DIM ** -0.5 * LOG2E
    half = SWA_HEAD_DIM // 2
    for hd in range(SWA_Q_HEADS):
        lo = hd * SWA_HEAD_DIM
        q1, q2 = _rope_rows(c_q[lo:lo + half], c_q[lo + half:lo + SWA_HEAD_DIM], c64, s64)
        sq_ref[hd, 0:half] = (q1 * swa_scale).astype(bf16)
        sq_ref[hd, half:SWA_HEAD_DIM] = (q2 * swa_scale).astype(bf16)
    for hd in range(SWA_KV_HEADS):
        lo = hd * SWA_HEAD_DIM
        k1, k2 = _rope_rows(c_k[lo:lo + half], c_k[lo + half:lo + SWA_HEAD_DIM], c64, s64)
        sk_ref[hd, 0:half] = k1.astype(bf16)
        sk_ref[hd, half:SWA_HEAD_DIM] = k2.astype(bf16)
        sv_ref[hd, 0:HEAD_V] = c_v[lo:lo + HEAD_V].astype(bf16)
        sv_ref[hd, HEAD_V:V_AUG] = ones

    na_scale = NA_HEAD_DIM ** -0.5 * LOG2E
    for hd in range(NA_HEADS):
        lo = hd * NA_HEAD_DIM
        nq_ref[hd] = (d_q[lo:lo + NA_HEAD_DIM] * na_scale).astype(bf16)
        nk_ref[hd] = d_k[lo:lo + NA_HEAD_DIM].astype(bf16)
        nv_ref[hd, 0:HEAD_V] = d_v[lo:lo + HEAD_V].astype(bf16)
        nv_ref[hd, HEAD_V:V_AUG] = ones


def _prep(xT, g1, w_inT, qg, w_uqT, kvg, w_ukT, w_uvT, c32, s32, c64, s64):
    S = xT.shape[1]
    tt = min(TOKEN_TILE, S)
    bf16 = jnp.bfloat16

    def tok(rows):
        return pl.BlockSpec((rows, tt), lambda i: (0, i))

    def head_out(heads, rows):
        return (jax.ShapeDtypeStruct((heads, rows, S), bf16),
                pl.BlockSpec((heads, rows, tt), lambda i: (0, 0, i)))

    outs = [head_out(MLA_HEADS, MLA_QK_DIM), head_out(MLA_HEADS, MLA_QK_DIM), head_out(MLA_HEADS, V_AUG),
            head_out(2 * DIFF_HEADS, DIFF_QK_DIM), head_out(2 * DIFF_HEADS, DIFF_QK_DIM),
            head_out(DIFF_HEADS, V_AUG),
            head_out(SWA_Q_HEADS, SWA_HEAD_DIM), head_out(SWA_KV_HEADS, SWA_HEAD_DIM),
            head_out(SWA_KV_HEADS, V_AUG),
            head_out(NA_HEADS, NA_HEAD_DIM), head_out(NA_HEADS, NA_HEAD_DIM), head_out(NA_HEADS, V_AUG)]
    return pl.pallas_call(
        _prep_kernel,
        grid=(S // tt,),
        in_specs=[tok(D_MODEL), _const_spec(g1.shape), _const_spec(w_inT.shape), _const_spec(qg.shape),
                  _const_spec(w_uqT.shape), _const_spec(kvg.shape), _const_spec(w_ukT.shape),
                  _const_spec(w_uvT.shape), tok(c32.shape[0]), tok(s32.shape[0]), tok(c64.shape[0]),
                  tok(s64.shape[0])],
        out_specs=[o[1] for o in outs],
        out_shape=[o[0] for o in outs],
        compiler_params=_params(1),
        name="prep",
    )(xT, g1, w_inT, qg, w_uqT, kvg, w_ukT, w_uvT, c32, s32, c64, s64)


def _dense_kernel(q_ref, qn_ref, k_ref, v_ref, o_ref, acc_ref, m_ref,
                  s0_ref, s1_ref, b0_ref, b1_ref):
    f32 = jnp.float32
    nk, tk = k_ref.shape[1], k_ref.shape[2]
    first_tile = pl.program_id(1) == 0
    q_cur, q_next = q_ref[0], qn_ref[0]
    sub = min(DENSE_SUB, tk)
    subs = [slice(c * sub, (c + 1) * sub) for c in range(tk // sub)]
    bufs = ((s0_ref, b0_ref), (s1_ref, b1_ref))

    def region(q_a, kb_a, kb_c, parity, new_tile=False):
        s_a, b_a = bufs[1 - parity]
        s_c, b_c = bufs[parity]
        if kb_c is not None:
            m_old = jnp.full(m_ref.shape, NEG_INF, f32) if new_tile else m_ref[...]
            m_new = jnp.maximum(m_old, b_c[...])
            alpha = jnp.exp2(m_old - m_new)
            m_ref[...] = m_new
        bmax = pv = None
        for c in subs:
            if kb_a is not None:
                s = jnp.dot(k_ref[0, kb_a, c, :], q_a, preferred_element_type=f32)
                s_a[c, :] = s
                mc = jnp.max(s, axis=0, keepdims=True)
                bmax = mc if bmax is None else jnp.maximum(bmax, mc)
            if kb_c is not None:
                p = jnp.exp2(s_c[c, :] - m_new).astype(jnp.bfloat16)
                t = jnp.dot(v_ref[0, kb_c, :, c], p, preferred_element_type=f32)
                pv = t if pv is None else pv + t
        if kb_a is not None:
            b_a[...] = bmax
        if kb_c is not None:
            acc_ref[...] = alpha * acc_ref[...] + pv

    @pl.when(first_tile)
    def _():
        acc_ref[...] = jnp.zeros_like(acc_ref)
        region(q_cur, 0, None, 1)

    region(q_cur, 1, 0, 0, new_tile=True)

    def steady(t, carry):
        for parity in (0, 1):
            @pl.when(t % 2 == parity)
            def _():
                region(q_cur, t + 1, t, parity)
        return carry

    jax.lax.fori_loop(1, nk - 1, steady, 0)
    region(q_next, 0, nk - 1, 1)
    acc = acc_ref[...]
    o_ref[0] = acc[0:HEAD_V] / acc[HEAD_V:HEAD_V + 1]
    acc_ref[...] = jnp.zeros_like(acc_ref)


def _dense_attention(qT, kT, vT, maps_per_value, name):
    M, d, S = qT.shape
    tq = min(DENSE_TQ, S)
    tk = min(DENSE_TK, S)
    nk, nq = S // tk, S // tq
    assert nk >= 2 and nk % 2 == 0
    k = jnp.swapaxes(kT, 1, 2).reshape(M, nk, tk, d)
    v = jnp.swapaxes(vT.reshape(vT.shape[0], V_AUG, nk, tk), 1, 2)
    row = pltpu.VMEM((1, tq), jnp.float32)
    return pl.pallas_call(
        _dense_kernel,
        grid=(M, nq),
        in_specs=[pl.BlockSpec((1, d, tq), lambda m, i: (m, 0, i)),
                  pl.BlockSpec((1, d, tq), lambda m, i: (m, 0, jnp.minimum(i + 1, nq - 1))),
                  pl.BlockSpec((1, nk, tk, d), lambda m, i: (m, 0, 0, 0)),
                  pl.BlockSpec((1, nk, V_AUG, tk), lambda m, i: (m // maps_per_value, 0, 0, 0))],
        out_specs=pl.BlockSpec((1, HEAD_V, tq), lambda m, i: (m, 0, i)),
        out_shape=jax.ShapeDtypeStruct((M, HEAD_V, S), jnp.float32),
        scratch_shapes=[pltpu.VMEM((V_AUG, tq), jnp.float32), row,
                        pltpu.VMEM((tk, tq), jnp.float32), pltpu.VMEM((tk, tq), jnp.float32),
                        row, row],
        compiler_params=_params(2),
        name=name,
    )(qT, qT, k, v)


def _local_kernel(sink_ref, q_ref, k0_ref, k1_ref, k2_ref, v0_ref, v1_ref, v2_ref, b_ref, o_ref,
                  *, has_sink, q_per_kv):
    f32 = jnp.float32
    n_heads, _, blk = q_ref.shape
    n_bias = b_ref.shape[1]
    for h in range(n_heads):
        q = q_ref[h]
        kv = h // q_per_kv
        s = [jnp.dot(k[kv], q, preferred_element_type=f32)
             + b_ref[0, h % n_bias, c * blk:(c + 1) * blk]
             for c, k in enumerate((k0_ref, k1_ref, k2_ref))]
        m = jnp.max(jnp.maximum(jnp.maximum(s[0], s[1]), s[2]), axis=0, keepdims=True)
        if has_sink:
            sink = sink_ref[h]
            m = jnp.maximum(m, sink)
        pv = None
        for sc, v in zip(s, (v0_ref, v1_ref, v2_ref)):
            p = jnp.exp2(sc - m).astype(jnp.bfloat16)
            t = jnp.dot(v[kv], p, preferred_element_type=f32)
            pv = t if pv is None else pv + t
        l = pv[HEAD_V:HEAD_V + 1]
        if has_sink:
            l = l + jnp.exp2(sink - m)
        o_ref[h] = pv[0:HEAD_V] / l


def _local_attention(qT, kT, vT, bias, sinks, q_per_kv, name):
    Hq, d, S = qT.shape
    blk = LOCAL_BLK
    nb = S // blk
    assert nb >= LOCAL_CHUNKS
    Hb = bias.shape[1]
    Hk = kT.shape[0]
    k = jnp.swapaxes(kT, 1, 2)
    has_sink = sinks is not None
    if not has_sink:
        sinks = jnp.zeros((Hq,), jnp.float32)

    def base(i):
        return jnp.clip(i - 1, 0, nb - LOCAL_CHUNKS)

    def case(i):
        return jnp.where(i == 0, 0, jnp.where(i == nb - 1, 2, 1))

    def kspec(c):
        return pl.BlockSpec((Hk, blk, d), lambda i: (0, base(i) + c, 0))

    def vspec(c):
        return pl.BlockSpec((Hk, V_AUG, blk), lambda i: (0, 0, base(i) + c))

    return pl.pallas_call(
        functools.partial(_local_kernel, has_sink=has_sink, q_per_kv=q_per_kv),
        grid=(nb,),
        in_specs=[pl.BlockSpec(memory_space=pltpu.SMEM),
                  pl.BlockSpec((Hq, d, blk), lambda i: (0, 0, i)),
                  kspec(0), kspec(1), kspec(2), vspec(0), vspec(1), vspec(2),
                  pl.BlockSpec((1, Hb, LOCAL_CHUNKS * blk, blk), lambda i: (case(i), 0, 0, 0))],
        out_specs=pl.BlockSpec((Hq, HEAD_V, blk), lambda i: (0, 0, i)),
        out_shape=jax.ShapeDtypeStruct((Hq, HEAD_V, S), jnp.float32),
        compiler_params=_params(1),
        name=name,
    )(sinks, qT, k, k, k, vT, vT, vT, bias)


def _swa_bias(nb):
    blk = LOCAL_BLK
    kk = np.arange(LOCAL_CHUNKS * blk)[:, None]
    qq = np.arange(blk)[None, :]
    out = []
    for q_off in (0, blk, 2 * blk):
        valid = np.abs(kk - (q_off + qq)) <= WINDOW
        out.append(np.where(valid, 0.0, NEG_INF))
    return jnp.asarray(np.stack(out)[:, None], jnp.float32)


def _na_bias(rpb, S):
    blk = LOCAL_BLK
    rows = S // GRID_W
    rpq = blk // GRID_W
    kpq = LOCAL_CHUNKS * rpq
    nb = S // blk
    n_off = 2 * NA_ROWS - 1
    kc = np.arange(GRID_W)[:, None]
    w = np.arange(GRID_W)[None, :]
    cs = np.clip(w - NA_COLS // 2, 0, GRID_W - NA_COLS)
    col_ok = (kc >= cs) & (kc < cs + NA_COLS)
    onehot = (kc - w + NA_COLS - 1)[None] == np.arange(2 * NA_COLS - 1)[:, None, None]
    onehot = jnp.asarray(onehot.reshape(2 * NA_COLS - 1, GRID_W * GRID_W), jnp.float32)
    block_of = np.full((3, kpq, rpq), n_off, np.int32)
    for c, i in enumerate((0, 1, nb - 1)):
        base_row = rpq * int(np.clip(i - 1, 0, nb - LOCAL_CHUNKS))
        for ki in range(kpq):
            for qj in range(rpq):
                r, kr = rpq * i + qj, base_row + ki
                rs = int(np.clip(r - NA_ROWS // 2, 0, rows - NA_ROWS))
                if rs <= kr < rs + NA_ROWS:
                    block_of[c, ki, qj] = kr - r + NA_ROWS - 1
    L, H = rpb.shape[:2]
    toe = jnp.einsum("lhab,bn->lhan", rpb, onehot, precision=jax.lax.Precision.HIGHEST)
    toe = toe.reshape(L, H, n_off, GRID_W, GRID_W) * LOG2E
    toe = jnp.where(jnp.asarray(col_ok), toe, NEG_INF)
    toe = jnp.concatenate([toe, jnp.full((L, H, 1, GRID_W, GRID_W), NEG_INF, jnp.float32)], axis=2)
    t = jnp.take(toe, jnp.asarray(block_of.reshape(-1)), axis=2)
    t = t.reshape(L, H, 3, kpq, rpq, GRID_W, GRID_W)
    t = jnp.transpose(t, (0, 2, 1, 3, 5, 4, 6))
    return t.reshape(L, 3, H, kpq * GRID_W, rpq * GRID_W)


def _post_kernel(x_ref, oa_ref, ob_ref, oc_ref, od_ref, ga_ref, gc_ref, gd_ref, gsub_ref,
                 lq1_ref, lk1_ref, lq2_ref, lk2_ref, linit_ref, w_out_ref, g2_ref, w_up_ref,
                 w_down_ref, gf_ref, y_ref, *, final):
    bf16 = jnp.bfloat16
    f32 = jnp.float32
    tokens = x_ref.shape[1]

    def heads(ref):
        return ref[...].reshape(ref.shape[0] * ref.shape[1], tokens)

    ya = _rms_rows(heads(oa_ref), ga_ref[...])
    yc = _rms_rows(heads(oc_ref), gc_ref[...])
    yd = _rms_rows(heads(od_ref), gd_ref[...])
    lam_init = linit_ref[...]
    lam = (jnp.exp(jnp.sum(lq1_ref[...] * lk1_ref[...], axis=1, keepdims=True))
           - jnp.exp(jnp.sum(lq2_ref[...] * lk2_ref[...], axis=1, keepdims=True)) + lam_init)
    yb = []
    for hd in range(DIFF_HEADS):
        w = ob_ref[2 * hd] - lam * ob_ref[2 * hd + 1]
        yb.append(_rms_rows(w, gsub_ref[...]) * (1.0 - lam_init))
    mix = jnp.concatenate([ya] + yb + [yc, yd], axis=0).astype(bf16)
    x1 = x_ref[...] + jnp.dot(w_out_ref[...], mix, preferred_element_type=f32)
    h2 = _rms_rows(x1, g2_ref[...]).astype(bf16)
    u = jnp.dot(w_up_ref[...], h2, preferred_element_type=f32)
    a = jnp.square(jnp.maximum(u, 0.0)).astype(bf16)
    x2 = x1 + jnp.dot(w_down_ref[...], a, preferred_element_type=f32)
    if final:
        x2 = _rms_rows(x2, gf_ref[...])
    y_ref[...] = x2


def _post(xT, oa, ob, oc, od, ga, gc, gd, gsub, lq1, lk1, lq2, lk2, linit, w_outT, g2, w_upT,
          w_downT, gf, final):
    S = xT.shape[1]
    tt = min(TOKEN_TILE, S)

    def tok(rows):
        return pl.BlockSpec((rows, tt), lambda i: (0, i))

    def heads(arr):
        return pl.BlockSpec((arr.shape[0], arr.shape[1], tt), lambda i: (0, 0, i))

    def resident(arr):
        nd = arr.ndim
        return pl.BlockSpec(arr.shape, lambda *_: (0,) * nd, pipeline_mode=pl.Buffered(1))

    small = [ga, gc, gd, gsub, lq1, lk1, lq2, lk2, linit]
    return pl.pallas_call(
        functools.partial(_post_kernel, final=final),
        grid=(S // tt,),
        in_specs=[tok(D_MODEL), heads(oa), heads(ob), heads(oc), heads(od)]
                 + [_const_spec(a.shape) for a in small]
                 + [resident(w_outT), _const_spec(g2.shape), resident(w_upT), resident(w_downT),
                    _const_spec(gf.shape)],
        out_specs=tok(D_MODEL),
        out_shape=jax.ShapeDtypeStruct((D_MODEL, S), jnp.float32),
        compiler_params=_params(1),
        name="post_final" if final else "post",
    )(xT, oa, ob, oc, od, *small, w_outT, g2, w_upT, w_downT, gf)


def _rope_tables_T(S, dim):
    inv = 1.0 / (ROPE_THETA ** (jnp.arange(0, dim, 2, dtype=jnp.float32) / dim))
    ang = inv[:, None] * jnp.arange(S, dtype=jnp.float32)[None, :]
    return jnp.cos(ang), jnp.sin(ang)


def _col(v):
    return v.astype(jnp.float32)[:, None]


def _wT(w):
    return jnp.swapaxes(w, -1, -2).astype(jnp.bfloat16)


@jax.jit
def _forward(x, norm1_g, w_in, mla_q_norm_g, mla_w_uq, mla_kv_norm_g, mla_w_uk, mla_w_uv,
             diff_lambda_q1, diff_lambda_k1, diff_lambda_q2, diff_lambda_k2, diff_subln_g,
             swa_sinks, na_rpb, out_g_mla, out_g_swa, out_g_na, w_out, norm2_g, w_up, w_down,
             final_norm_g):
    B, S, D = x.shape
    assert B == 1 and D == D_MODEL and S % TOKEN_TILE == 0 and S % GRID_W == 0
    c32, s32 = _rope_tables_T(S, MLA_ROPE_DIM)
    c64, s64 = _rope_tables_T(S, SWA_HEAD_DIM)
    swa_bias = _swa_bias(S // LOCAL_BLK)
    na_bias = _na_bias(na_rpb.astype(jnp.float32), S)
    xT = x[0].T
    gf = _col(final_norm_g)
    for l in range(DEPTH):
        prep = _prep(xT, _col(norm1_g[l]), _wT(w_in[l]), _col(mla_q_norm_g[l]), _wT(mla_w_uq[l]),
                     _col(mla_kv_norm_g[l]), _wT(mla_w_uk[l]), _wT(mla_w_uv[l]), c32, s32, c64, s64)
        mq, mk, mv, dq, dk, dv, sq, sk, sv, nq, nk, nv = prep
        oa = _dense_attention(mq, mk, mv, 1, "dense_mla")
        ob = _dense_attention(dq, dk, dv, 2, "dense_diff")
        oc = _local_attention(sq, sk, sv, swa_bias, swa_sinks[l].astype(jnp.float32) * LOG2E,
                              SWA_GROUP, "local_swa")
        od = _local_attention(nq, nk, nv, na_bias[l], None, 1, "local_na")
        lam_init = 0.8 - 0.6 * math.exp(-0.3 * l)
        row = lambda v: v.astype(jnp.float32)[None, :]
        xT = _post(xT, oa, ob, oc, od, _col(out_g_mla[l]), _col(out_g_swa[l]), _col(out_g_na[l]),
                   _col(diff_subln_g[l]), row(diff_lambda_q1[l]), row(diff_lambda_k1[l]),
                   row(diff_lambda_q2[l]), row(diff_lambda_k2[l]),
                   jnp.full((1, 1), lam_init, jnp.float32), _wT(w_out[l]), _col(norm2_g[l]),
                   _wT(w_up[l]), _wT(w_down[l]), gf, final=(l == DEPTH - 1))
    return xT.T[None]


def kernel(x, norm1_g, w_in, mla_q_norm_g, mla_w_uq, mla_kv_norm_g, mla_w_uk, mla_w_uv,
           diff_lambda_q1, diff_lambda_k1, diff_lambda_q2, diff_lambda_k2, diff_subln_g,
           swa_sinks, na_rpb, out_g_mla, out_g_swa, out_g_na, w_out, norm2_g, w_up, w_down,
           final_norm_g):
    return _forward(x, norm1_g, w_in, mla_q_norm_g, mla_w_uq, mla_kv_norm_g, mla_w_uk, mla_w_uv,
                    diff_lambda_q1, diff_lambda_k1, diff_lambda_q2, diff_lambda_k2, diff_subln_g,
                    swa_sinks, na_rpb, out_g_mla, out_g_swa, out_g_na, w_out, norm2_g, w_up,
                    w_down, final_norm_g)
```

```python
import functools
import math

import jax
import jax.numpy as jnp
import numpy as np
from jax.experimental import pallas as pl
from jax.experimental.pallas import tpu as pltpu

D_MODEL = 1024
DEPTH = 4
GRID_W = 64
ROPE_THETA = 10000.0
EPS = 1e-6
NEG_INF = -1e30

MLA_HEADS = 4
MLA_Q_RANK = 192
MLA_KV_RANK = 128
MLA_NOPE_DIM = 64
MLA_ROPE_DIM = 32
MLA_V_DIM = 64
MLA_QK_DIM = MLA_NOPE_DIM + MLA_ROPE_DIM
DIFF_HEADS = 4
DIFF_QK_DIM = 32
DIFF_V_DIM = 64
SWA_Q_HEADS = 4
SWA_KV_HEADS = 2
SWA_GROUP = SWA_Q_HEADS // SWA_KV_HEADS
SWA_HEAD_DIM = 64
WINDOW = 128
NA_HEADS = 4
NA_HEAD_DIM = 64
NA_ROWS = 8
NA_COLS = 16
D_FF = 4 * D_MODEL
HEAD_V = 64
GROUP_OUT = 256

IN_SPLIT_WIDTHS = (
    MLA_Q_RANK, MLA_KV_RANK, MLA_ROPE_DIM,
    DIFF_HEADS * 2 * DIFF_QK_DIM, DIFF_HEADS * 2 * DIFF_QK_DIM, DIFF_HEADS * DIFF_V_DIM,
    SWA_Q_HEADS * SWA_HEAD_DIM, SWA_KV_HEADS * SWA_HEAD_DIM, SWA_KV_HEADS * SWA_HEAD_DIM,
    NA_HEADS * NA_HEAD_DIM, NA_HEADS * NA_HEAD_DIM, NA_HEADS * NA_HEAD_DIM,
)
IN_COLS = sum(IN_SPLIT_WIDTHS)
IN_OFFSETS = tuple(int(v) for v in np.cumsum((0,) + IN_SPLIT_WIDTHS))

V7X_LANES = 128
V7X_BF16_SUBLANES = 16
V7X_VMEM_LIMIT_BYTES = 56 * 1024 * 1024

LOG2E = math.log2(math.e)
V_AUG = HEAD_V + V7X_BF16_SUBLANES

TOKEN_TILE = 512
DENSE_TQ = 512
DENSE_TK = 8192
DENSE_SUB = 256
LOCAL_BLK = 256
LOCAL_CHUNKS = 3


def _params(n_axes):
    return pltpu.CompilerParams(
        dimension_semantics=("arbitrary",) * n_axes,
        vmem_limit_bytes=V7X_VMEM_LIMIT_BYTES,
    )


def _const_spec(shape):
    nd = len(shape)
    return pl.BlockSpec(shape, lambda *_: (0,) * nd)


def _rms_rows(x, g):
    r = jax.lax.rsqrt(jnp.mean(x * x, axis=0, keepdims=True) + EPS)
    return (x * r) * g


def _rope_rows(x1, x2, c, s):
    return x1 * c - x2 * s, x2 * c + x1 * s


def _ones_rows(tokens):
    row = jax.lax.broadcasted_iota(jnp.int32, (V7X_BF16_SUBLANES, tokens), 0)
    return jnp.where(row == 0, 1.0, 0.0).astype(jnp.bfloat16)


def _prep_kernel(x_ref, g1_ref, w_in_ref, qg_ref, w_uq_ref, kvg_ref, w_uk_ref, w_uv_ref,
                 c32_ref, s32_ref, c64_ref, s64_ref,
                 mq_ref, mk_ref, mv_ref, dq_ref, dk_ref, dv_ref,
                 sq_ref, sk_ref, sv_ref, nq_ref, nk_ref, nv_ref):
    bf16 = jnp.bfloat16
    f32 = jnp.float32
    tokens = x_ref.shape[1]
    h = _rms_rows(x_ref[...], g1_ref[...]).astype(bf16)
    proj = jnp.dot(w_in_ref[...], h, preferred_element_type=f32)
    part = [proj[IN_OFFSETS[i]:IN_OFFSETS[i + 1]] for i in range(len(IN_SPLIT_WIDTHS))]
    a_cq, a_ckv, a_kr, b_q, b_k, b_v, c_q, c_k, c_v, d_q, d_k, d_v = part
    c32, s32 = c32_ref[...], s32_ref[...]
    c64, s64 = c64_ref[...], s64_ref[...]
    ones = _ones_rows(tokens)

    mla_scale = MLA_QK_DIM ** -0.5 * LOG2E
    cqn = _rms_rows(a_cq, qg_ref[...]).astype(bf16)
    q_all = jnp.dot(w_uq_ref[...], cqn, preferred_element_type=f32)
    lat = _rms_rows(a_ckv, kvg_ref[...]).astype(bf16)
    k_nope = jnp.dot(w_uk_ref[...], lat, preferred_element_type=f32)
    v_all = jnp.dot(w_uv_ref[...], lat, preferred_element_type=f32)
    half = MLA_ROPE_DIM // 2
    kp1, kp2 = _rope_rows(a_kr[:half], a_kr[half:], c32, s32)
    for hd in range(MLA_HEADS):
        q = q_all[hd * MLA_QK_DIM:(hd + 1) * MLA_QK_DIM]
        q1, q2 = _rope_rows(q[MLA_NOPE_DIM:MLA_NOPE_DIM + half], q[MLA_NOPE_DIM + half:], c32, s32)
        mq_ref[hd, 0:MLA_NOPE_DIM] = (q[:MLA_NOPE_DIM] * mla_scale).astype(bf16)
        mq_ref[hd, MLA_NOPE_DIM:MLA_NOPE_DIM + half] = (q1 * mla_scale).astype(bf16)
        mq_ref[hd, MLA_NOPE_DIM + half:MLA_QK_DIM] = (q2 * mla_scale).astype(bf16)
        mk_ref[hd, 0:MLA_NOPE_DIM] = k_nope[hd * MLA_NOPE_DIM:(hd + 1) * MLA_NOPE_DIM].astype(bf16)
        mk_ref[hd, MLA_NOPE_DIM:MLA_NOPE_DIM + half] = kp1.astype(bf16)
        mk_ref[hd, MLA_NOPE_DIM + half:MLA_QK_DIM] = kp2.astype(bf16)
        mv_ref[hd, 0:HEAD_V] = v_all[hd * HEAD_V:(hd + 1) * HEAD_V].astype(bf16)
        mv_ref[hd, HEAD_V:V_AUG] = ones

    diff_scale = DIFF_QK_DIM ** -0.5 * LOG2E
    half = DIFF_QK_DIM // 2
    for m in range(2 * DIFF_HEADS):
        lo = m * DIFF_QK_DIM
        q1, q2 = _rope_rows(b_q[lo:lo + half], b_q[lo + half:lo + DIFF_QK_DIM], c32, s32)
        k1, k2 = _rope_rows(b_k[lo:lo + half], b_k[lo + half:lo + DIFF_QK_DIM], c32, s32)
        dq_ref[m, 0:half] = (q1 * diff_scale).astype(bf16)
        dq_ref[m, half:DIFF_QK_DIM] = (q2 * diff_scale).astype(bf16)
        dk_ref[m, 0:half] = k1.astype(bf16)
        dk_ref[m, half:DIFF_QK_DIM] = k2.astype(bf16)
    for hd in range(DIFF_HEADS):
        dv_ref[hd, 0:HEAD_V] = b_v[hd * HEAD_V:(hd + 1) * HEAD_V].astype(bf16)
        dv_ref[hd, HEAD_V:V_AUG] = ones

    swa_scale = SWA_HEAD_DIM ** -0.5 * LOG2E
    half = SWA_HEAD_DIM // 2
    for hd in range(SWA_Q_HEADS):
        lo = hd * SWA_HEAD_DIM
        q1, q2 = _rope_rows(c_q[lo:lo + half], c_q[lo + half:lo + SWA_HEAD_DIM], c64, s64)
        sq_ref[hd, 0:half] = (q1 * swa_scale).astype(bf16)
        sq_ref[hd, half:SWA_HEAD_DIM] = (q2 * swa_scale).astype(bf16)
    for hd in range(SWA_KV_HEADS):
        lo = hd * SWA_HEAD_DIM
        k1, k2 = _rope_rows(c_k[lo:lo + half], c_k[lo + half:lo + SWA_HEAD_DIM], c64, s64)
        sk_ref[hd, 0:half] = k1.astype(bf16)
        sk_ref[hd, half:SWA_HEAD_DIM] = k2.astype(bf16)
        sv_ref[hd, 0:HEAD_V] = c_v[lo:lo + HEAD_V].astype(bf16)
        sv_ref[hd, HEAD_V:V_AUG] = ones

    na_scale = NA_HEAD_DIM ** -0.5 * LOG2E
    for hd in range(NA_HEADS):
        lo = hd * NA_HEAD_DIM
        nq_ref[hd] = (d_q[lo:lo + NA_HEAD_DIM] * na_scale).astype(bf16)
        nk_ref[hd] = d_k[lo:lo + NA_HEAD_DIM].astype(bf16)
        nv_ref[hd, 0:HEAD_V] = d_v[lo:lo + HEAD_V].astype(bf16)
        nv_ref[hd, HEAD_V:V_AUG] = ones


def _prep(xT, g1, w_inT, qg, w_uqT, kvg, w_ukT, w_uvT, c32, s32, c64, s64):
    S = xT.shape[1]
    tt = min(TOKEN_TILE, S)
    bf16 = jnp.bfloat16

    def tok(rows):
        return pl.BlockSpec((rows, tt), lambda i: (0, i))

    def head_out(heads, rows):
        return (jax.ShapeDtypeStruct((heads, rows, S), bf16),
                pl.BlockSpec((heads, rows, tt), lambda i: (0, 0, i)))

    outs = [head_out(MLA_HEADS, MLA_QK_DIM), head_out(MLA_HEADS, MLA_QK_DIM), head_out(MLA_HEADS, V_AUG),
            head_out(2 * DIFF_HEADS, DIFF_QK_DIM), head_out(2 * DIFF_HEADS, DIFF_QK_DIM),
            head_out(DIFF_HEADS, V_AUG),
            head_out(SWA_Q_HEADS, SWA_HEAD_DIM), head_out(SWA_KV_HEADS, SWA_HEAD_DIM),
            head_out(SWA_KV_HEADS, V_AUG),
            head_out(NA_HEADS, NA_HEAD_DIM), head_out(NA_HEADS, NA_HEAD_DIM), head_out(NA_HEADS, V_AUG)]
    return pl.pallas_call(
        _prep_kernel,
        grid=(S // tt,),
        in_specs=[tok(D_MODEL), _const_spec(g1.shape), _const_spec(w_inT.shape), _const_spec(qg.shape),
                  _const_spec(w_uqT.shape), _const_spec(kvg.shape), _const_spec(w_ukT.shape),
                  _const_spec(w_uvT.shape), tok(c32.shape[0]), tok(s32.shape[0]), tok(c64.shape[0]),
                  tok(s64.shape[0])],
        out_specs=[o[1] for o in outs],
        out_shape=[o[0] for o in outs],
        compiler_params=_params(1),
        name="prep",
    )(xT, g1, w_inT, qg, w_uqT, kvg, w_ukT, w_uvT, c32, s32, c64, s64)


def _dense_kernel(q_ref, qn_ref, k_ref, v_ref, o_ref, acc_ref, m_ref,
                  s0_ref, s1_ref, b0_ref, b1_ref, qq_ref):
    f32 = jnp.float32
    nk, tk = k_ref.shape[1], k_ref.shape[2]
    first_tile = pl.program_id(1) == 0
    qq_ref[0] = q_ref[0]
    qq_ref[1] = qn_ref[0]
    sub = min(DENSE_SUB, tk)
    subs = [slice(c * sub, (c + 1) * sub) for c in range(tk // sub)]
    bufs = ((s0_ref, b0_ref), (s1_ref, b1_ref))

    def region(q_a, kb_a, kb_c, parity, m_old):
        s_a, b_a = bufs[1 - parity]
        s_c, b_c = bufs[parity]
        if kb_c is not None:
            m_new = jnp.maximum(m_old, b_c[...])
            alpha = jnp.exp2(m_old - m_new)
            m_ref[...] = m_new
        bmax = pv = None
        for c in subs:
            s = jnp.dot(k_ref[0, kb_a, c, :], q_a, preferred_element_type=f32)
            s_a[c, :] = s
            mc = jnp.max(s, axis=0, keepdims=True)
            bmax = mc if bmax is None else jnp.maximum(bmax, mc)
            if kb_c is not None:
                p = jnp.exp2(s_c[c, :] - m_new).astype(jnp.bfloat16)
                t = jnp.dot(v_ref[0, kb_c, :, c], p, preferred_element_type=f32)
                pv = t if pv is None else pv + t
        b_a[...] = bmax
        if kb_c is not None:
            acc_ref[...] = alpha * acc_ref[...] + pv

    @pl.when(first_tile)
    def _():
        acc_ref[...] = jnp.zeros_like(acc_ref)
        region(q_ref[0], 0, None, 1, None)

    def block(t, carry):
        last = t == nk - 1
        for parity in (0, 1):
            @pl.when(t % 2 == parity)
            def _():
                q_a = qq_ref[last.astype(jnp.int32)]
                kb_a = jnp.where(last, 0, t + 1)
                m_old = jnp.where(t == 0, NEG_INF, m_ref[...])
                region(q_a, kb_a, t, parity, m_old)
        return carry

    jax.lax.fori_loop(0, nk, block, 0)
    acc = acc_ref[...]
    o_ref[0] = acc[0:HEAD_V] / acc[HEAD_V:HEAD_V + 1]
    acc_ref[...] = jnp.zeros_like(acc_ref)


def _dense_attention(qT, kT, vT, maps_per_value, name):
    M, d, S = qT.shape
    tq = min(DENSE_TQ, S)
    tk = min(DENSE_TK, S)
    nk, nq = S // tk, S // tq
    assert nk >= 2 and nk % 2 == 0
    k = jnp.swapaxes(kT, 1, 2).reshape(M, nk, tk, d)
    v = jnp.swapaxes(vT.reshape(vT.shape[0], V_AUG, nk, tk), 1, 2)
    row = pltpu.VMEM((1, tq), jnp.float32)
    return pl.pallas_call(
        _dense_kernel,
        grid=(M, nq),
        in_specs=[pl.BlockSpec((1, d, tq), lambda m, i: (m, 0, i)),
                  pl.BlockSpec((1, d, tq), lambda m, i: (m, 0, jnp.minimum(i + 1, nq - 1))),
                  pl.BlockSpec((1, nk, tk, d), lambda m, i: (m, 0, 0, 0)),
                  pl.BlockSpec((1, nk, V_AUG, tk), lambda m, i: (m // maps_per_value, 0, 0, 0))],
        out_specs=pl.BlockSpec((1, HEAD_V, tq), lambda m, i: (m, 0, i)),
        out_shape=jax.ShapeDtypeStruct((M, HEAD_V, S), jnp.float32),
        scratch_shapes=[pltpu.VMEM((V_AUG, tq), jnp.float32), row,
                        pltpu.VMEM((tk, tq), jnp.float32), pltpu.VMEM((tk, tq), jnp.float32),
                        row, row, pltpu.VMEM((2, d, tq), qT.dtype)],
        compiler_params=_params(2),
        name=name,
    )(qT, qT, k, v)


def _local_kernel(sink_ref, *refs, groups):
    f32 = jnp.float32
    n_in = 8
    outs = refs[n_in * len(groups):]
    n_heads = max(refs[n_in * g].shape[0] for g in range(len(groups)))
    for h in range(n_heads):
        for g, (has_sink, q_per_kv) in enumerate(groups):
            q_ref, k0_ref, k1_ref, k2_ref, v0_ref, v1_ref, v2_ref, b_ref = refs[n_in * g:n_in * (g + 1)]
            if h >= q_ref.shape[0]:
                continue
            blk = q_ref.shape[2]
            q = q_ref[h]
            kv = h // q_per_kv
            s = [jnp.dot(k[kv], q, preferred_element_type=f32)
                 + b_ref[0, h % b_ref.shape[1], c * blk:(c + 1) * blk]
                 for c, k in enumerate((k0_ref, k1_ref, k2_ref))]
            m = jnp.max(jnp.maximum(jnp.maximum(s[0], s[1]), s[2]), axis=0, keepdims=True)
            if has_sink:
                sink = sink_ref[h]
                m = jnp.maximum(m, sink)
            pv = None
            for sc, v in zip(s, (v0_ref, v1_ref, v2_ref)):
                p = jnp.exp2(sc - m).astype(jnp.bfloat16)
                t = jnp.dot(v[kv], p, preferred_element_type=f32)
                pv = t if pv is None else pv + t
            l = pv[HEAD_V:HEAD_V + 1]
            if has_sink:
                l = l + jnp.exp2(sink - m)
            outs[g][h] = pv[0:HEAD_V] / l


def _local_attention(sinks, group_args):
    S = group_args[0][0].shape[2]
    blk = LOCAL_BLK
    nb = S // blk
    assert nb >= LOCAL_CHUNKS

    def base(i):
        return jnp.clip(i - 1, 0, nb - LOCAL_CHUNKS)

    def case(i):
        return jnp.where(i == 0, 0, jnp.where(i == nb - 1, 2, 1))

    operands, in_specs, out_specs, out_shapes, groups = [sinks], [pl.BlockSpec(memory_space=pltpu.SMEM)], [], [], []
    for qT, kT, vT, bias, q_per_kv, has_sink in group_args:
        Hq, d, _ = qT.shape
        Hk, Hb = kT.shape[0], bias.shape[1]
        k = jnp.swapaxes(kT, 1, 2)
        operands += [qT, k, k, k, vT, vT, vT, bias]
        in_specs.append(pl.BlockSpec((Hq, d, blk), lambda i: (0, 0, i)))
        in_specs += [pl.BlockSpec((Hk, blk, d), functools.partial(lambda c, i: (0, base(i) + c, 0), c))
                     for c in range(LOCAL_CHUNKS)]
        in_specs += [pl.BlockSpec((Hk, V_AUG, blk), functools.partial(lambda c, i: (0, 0, base(i) + c), c))
                     for c in range(LOCAL_CHUNKS)]
        in_specs.append(pl.BlockSpec((1, Hb, LOCAL_CHUNKS * blk, blk), lambda i: (case(i), 0, 0, 0)))
        out_specs.append(pl.BlockSpec((Hq, HEAD_V, blk), lambda i: (0, 0, i)))
        out_shapes.append(jax.ShapeDtypeStruct((Hq, HEAD_V, S), jnp.float32))
        groups.append((has_sink, q_per_kv))
    return pl.pallas_call(
        functools.partial(_local_kernel, groups=tuple(groups)),
        grid=(nb,),
        in_specs=in_specs,
        out_specs=out_specs,
        out_shape=out_shapes,
        compiler_params=_params(1),
        name="local",
    )(*operands)


def _swa_bias(nb):
    blk = LOCAL_BLK
    kk = np.arange(LOCAL_CHUNKS * blk)[:, None]
    qq = np.arange(blk)[None, :]
    out = []
    for q_off in (0, blk, 2 * blk):
        valid = np.abs(kk - (q_off + qq)) <= WINDOW
        out.append(np.where(valid, 0.0, NEG_INF))
    return jnp.asarray(np.stack(out)[:, None], jnp.float32)


def _na_bias(rpb, S):
    blk = LOCAL_BLK
    rows = S // GRID_W
    rpq = blk // GRID_W
    kpq = LOCAL_CHUNKS * rpq
    nb = S // blk
    n_off = 2 * NA_ROWS - 1
    kc = np.arange(GRID_W)[:, None]
    w = np.arange(GRID_W)[None, :]
    cs = np.clip(w - NA_COLS // 2, 0, GRID_W - NA_COLS)
    col_ok = (kc >= cs) & (kc < cs + NA_COLS)
    onehot = (kc - w + NA_COLS - 1)[None] == np.arange(2 * NA_COLS - 1)[:, None, None]
    onehot = jnp.asarray(onehot.reshape(2 * NA_COLS - 1, GRID_W * GRID_W), jnp.float32)
    block_of = np.full((3, kpq, rpq), n_off, np.int32)
    for c, i in enumerate((0, 1, nb - 1)):
        base_row = rpq * int(np.clip(i - 1, 0, nb - LOCAL_CHUNKS))
        for ki in range(kpq):
            for qj in range(rpq):
                r, kr = rpq * i + qj, base_row + ki
                rs = int(np.clip(r - NA_ROWS // 2, 0, rows - NA_ROWS))
                if rs <= kr < rs + NA_ROWS:
                    block_of[c, ki, qj] = kr - r + NA_ROWS - 1
    L, H = rpb.shape[:2]
    toe = jnp.einsum("lhab,bn->lhan", rpb, onehot, precision=jax.lax.Precision.HIGHEST)
    toe = toe.reshape(L, H, n_off, GRID_W, GRID_W) * LOG2E
    toe = jnp.where(jnp.asarray(col_ok), toe, NEG_INF)
    toe = jnp.concatenate([toe, jnp.full((L, H, 1, GRID_W, GRID_W), NEG_INF, jnp.float32)], axis=2)
    t = jnp.take(toe, jnp.asarray(block_of.reshape(-1)), axis=2)
    t = t.reshape(L, H, 3, kpq, rpq, GRID_W, GRID_W)
    t = jnp.transpose(t, (0, 2, 1, 3, 5, 4, 6))
    return t.reshape(L, 3, H, kpq * GRID_W, rpq * GRID_W)


def _post_kernel(x_ref, oa_ref, ob_ref, oc_ref, od_ref, ga_ref, gc_ref, gd_ref, gsub_ref,
                 lq1_ref, lk1_ref, lq2_ref, lk2_ref, linit_ref, w_out_ref, g2_ref, w_up_ref,
                 w_down_ref, gf_ref, y_ref, *, final):
    bf16 = jnp.bfloat16
    f32 = jnp.float32
    tokens = x_ref.shape[1]

    def heads(ref):
        return ref[...].reshape(ref.shape[0] * ref.shape[1], tokens)

    ya = _rms_rows(heads(oa_ref), ga_ref[...])
    yc = _rms_rows(heads(oc_ref), gc_ref[...])
    yd = _rms_rows(heads(od_ref), gd_ref[...])
    lam_init = linit_ref[...]
    lam = (jnp.exp(jnp.sum(lq1_ref[...] * lk1_ref[...], axis=1, keepdims=True))
           - jnp.exp(jnp.sum(lq2_ref[...] * lk2_ref[...], axis=1, keepdims=True)) + lam_init)
    yb = []
    for hd in range(DIFF_HEADS):
        w = ob_ref[2 * hd] - lam * ob_ref[2 * hd + 1]
        yb.append(_rms_rows(w, gsub_ref[...]) * (1.0 - lam_init))
    mix = jnp.concatenate([ya] + yb + [yc, yd], axis=0).astype(bf16)
    x1 = x_ref[...] + jnp.dot(w_out_ref[...], mix, preferred_element_type=f32)
    h2 = _rms_rows(x1, g2_ref[...]).astype(bf16)
    u = jnp.dot(w_up_ref[...], h2, preferred_element_type=f32)
    a = jnp.square(jnp.maximum(u, 0.0)).astype(bf16)
    x2 = x1 + jnp.dot(w_down_ref[...], a, preferred_element_type=f32)
    if final:
        x2 = _rms_rows(x2, gf_ref[...])
    y_ref[...] = x2


def _post(xT, oa, ob, oc, od, ga, gc, gd, gsub, lq1, lk1, lq2, lk2, linit, w_outT, g2, w_upT,
          w_downT, gf, final):
    S = xT.shape[1]
    tt = min(TOKEN_TILE, S)

    def tok(rows):
        return pl.BlockSpec((rows, tt), lambda i: (0, i))

    def heads(arr):
        return pl.BlockSpec((arr.shape[0], arr.shape[1], tt), lambda i: (0, 0, i))

    def resident(arr):
        nd = arr.ndim
        return pl.BlockSpec(arr.shape, lambda *_: (0,) * nd, pipeline_mode=pl.Buffered(1))

    small = [ga, gc, gd, gsub, lq1, lk1, lq2, lk2, linit]
    return pl.pallas_call(
        functools.partial(_post_kernel, final=final),
        grid=(S // tt,),
        in_specs=[tok(D_MODEL), heads(oa), heads(ob), heads(oc), heads(od)]
                 + [_const_spec(a.shape) for a in small]
                 + [resident(w_outT), _const_spec(g2.shape), resident(w_upT), resident(w_downT),
                    _const_spec(gf.shape)],
        out_specs=tok(D_MODEL),
        out_shape=jax.ShapeDtypeStruct((D_MODEL, S), jnp.float32),
        compiler_params=_params(1),
        name="post_final" if final else "post",
    )(xT, oa, ob, oc, od, *small, w_outT, g2, w_upT, w_downT, gf)


def _rope_tables_T(S, dim):
    inv = 1.0 / (ROPE_THETA ** (jnp.arange(0, dim, 2, dtype=jnp.float32) / dim))
    ang = inv[:, None] * jnp.arange(S, dtype=jnp.float32)[None, :]
    return jnp.cos(ang), jnp.sin(ang)


def _col(v):
    return v.astype(jnp.float32)[:, None]


def _wT(w):
    return jnp.swapaxes(w, -1, -2).astype(jnp.bfloat16)


@jax.jit
def _forward(x, norm1_g, w_in, mla_q_norm_g, mla_w_uq, mla_kv_norm_g, mla_w_uk, mla_w_uv,
             diff_lambda_q1, diff_lambda_k1, diff_lambda_q2, diff_lambda_k2, diff_subln_g,
             swa_sinks, na_rpb, out_g_mla, out_g_swa, out_g_na, w_out, norm2_g, w_up, w_down,
             final_norm_g):
    B, S, D = x.shape
    assert B == 1 and D == D_MODEL and S % TOKEN_TILE == 0 and S % GRID_W == 0
    c32, s32 = _rope_tables_T(S, MLA_ROPE_DIM)
    c64, s64 = _rope_tables_T(S, SWA_HEAD_DIM)
    swa_bias = _swa_bias(S // LOCAL_BLK)
    na_bias = _na_bias(na_rpb.astype(jnp.float32), S)
    xT = x[0].T
    gf = _col(final_norm_g)
    for l in range(DEPTH):
        prep = _prep(xT, _col(norm1_g[l]), _wT(w_in[l]), _col(mla_q_norm_g[l]), _wT(mla_w_uq[l]),
                     _col(mla_kv_norm_g[l]), _wT(mla_w_uk[l]), _wT(mla_w_uv[l]), c32, s32, c64, s64)
        mq, mk, mv, dq, dk, dv, sq, sk, sv, nq, nk, nv = prep
        oa = _dense_attention(mq, mk, mv, 1, "dense_mla")
        ob = _dense_attention(dq, dk, dv, 2, "dense_diff")
        oc, od = _local_attention(swa_sinks[l].astype(jnp.float32) * LOG2E,
                                  [(sq, sk, sv, swa_bias, SWA_GROUP, True),
                                   (nq, nk, nv, na_bias[l], 1, False)])
        lam_init = 0.8 - 0.6 * math.exp(-0.3 * l)
        row = lambda v: v.astype(jnp.float32)[None, :]
        xT = _post(xT, oa, ob, oc, od, _col(out_g_mla[l]), _col(out_g_swa[l]), _col(out_g_na[l]),
                   _col(diff_subln_g[l]), row(diff_lambda_q1[l]), row(diff_lambda_k1[l]),
                   row(diff_lambda_q2[l]), row(diff_lambda_k2[l]),
                   jnp.full((1, 1), lam_init, jnp.float32), _wT(w_out[l]), _col(norm2_g[l]),
                   _wT(w_up[l]), _wT(w_down[l]), gf, final=(l == DEPTH - 1))
    return xT.T[None]


def kernel(x, norm1_g, w_in, mla_q_norm_g, mla_w_uq, mla_kv_norm_g, mla_w_uk, mla_w_uv,
           diff_lambda_q1, diff_lambda_k1, diff_lambda_q2, diff_lambda_k2, diff_subln_g,
           swa_sinks, na_rpb, out_g_mla, out_g_swa, out_g_na, w_out, norm2_g, w_up, w_down,
           final_norm_g):
    return _forward(x, norm1_g, w_in, mla_q_norm_g, mla_w_uq, mla_kv_norm_g, mla_w_uk, mla_w_uv,
                    diff_lambda_q1, diff_lambda_k1, diff_lambda_q2, diff_lambda_k2, diff_subln_g,
                    swa_sinks, na_rpb, out_g_mla, out_g_swa, out_g_na, w_out, norm2_g, w_up,
                    w_down, final_norm_g)
```

```python
import functools
import math

import jax
import jax.numpy as jnp
import numpy as np
from jax.experimental import pallas as pl
from jax.experimental.pallas import tpu as pltpu

D_MODEL = 1024
DEPTH = 4
GRID_W = 64
ROPE_THETA = 10000.0
EPS = 1e-6
NEG_INF = -1e30

MLA_HEADS = 4
MLA_Q_RANK = 192
MLA_KV_RANK = 128
MLA_NOPE_DIM = 64
MLA_ROPE_DIM = 32
MLA_V_DIM = 64
MLA_QK_DIM = MLA_NOPE_DIM + MLA_ROPE_DIM
DIFF_HEADS = 4
DIFF_QK_DIM = 32
DIFF_V_DIM = 64
SWA_Q_HEADS = 4
SWA_KV_HEADS = 2
SWA_GROUP = SWA_Q_HEADS // SWA_KV_HEADS
SWA_HEAD_DIM = 64
WINDOW = 128
NA_HEADS = 4
NA_HEAD_DIM = 64
NA_ROWS = 8
NA_COLS = 16
D_FF = 4 * D_MODEL
HEAD_V = 64
GROUP_OUT = 256

IN_SPLIT_WIDTHS = (
    MLA_Q_RANK, MLA_KV_RANK, MLA_ROPE_DIM,
    DIFF_HEADS * 2 * DIFF_QK_DIM, DIFF_HEADS * 2 * DIFF_QK_DIM, DIFF_HEADS * DIFF_V_DIM,
    SWA_Q_HEADS * SWA_HEAD_DIM, SWA_KV_HEADS * SWA_HEAD_DIM, SWA_KV_HEADS * SWA_HEAD_DIM,
    NA_HEADS * NA_HEAD_DIM, NA_HEADS * NA_HEAD_DIM, NA_HEADS * NA_HEAD_DIM,
)
IN_COLS = sum(IN_SPLIT_WIDTHS)
IN_OFFSETS = tuple(int(v) for v in np.cumsum((0,) + IN_SPLIT_WIDTHS))

V7X_LANES = 128
V7X_BF16_SUBLANES = 16
V7X_VMEM_LIMIT_BYTES = 56 * 1024 * 1024

LOG2E = math.log2(math.e)
V_AUG = HEAD_V + V7X_BF16_SUBLANES

TOKEN_TILE = 512
DENSE_TQ = 512
DENSE_TK = 8192
DENSE_SUB = 256
LOCAL_BLK = 256
LOCAL_CHUNKS = 3


def _params(n_axes):
    return pltpu.CompilerParams(
        dimension_semantics=("arbitrary",) * n_axes,
        vmem_limit_bytes=V7X_VMEM_LIMIT_BYTES,
    )


def _const_spec(shape):
    nd = len(shape)
    return pl.BlockSpec(shape, lambda *_: (0,) * nd)


def _rms_rows(x, g):
    r = jax.lax.rsqrt(jnp.mean(x * x, axis=0, keepdims=True) + EPS)
    return (x * r) * g


def _rope_rows(x1, x2, c, s):
    return x1 * c - x2 * s, x2 * c + x1 * s


def _ones_rows(tokens):
    row = jax.lax.broadcasted_iota(jnp.int32, (V7X_BF16_SUBLANES, tokens), 0)
    return jnp.where(row == 0, 1.0, 0.0).astype(jnp.bfloat16)


def _prep_kernel(x_ref, g1_ref, w_in_ref, qg_ref, w_uq_ref, kvg_ref, w_uk_ref, w_uv_ref,
                 c32_ref, s32_ref, c64_ref, s64_ref,
                 mq_ref, mk_ref, mv_ref, dq_ref, dk_ref, dv_ref,
                 sq_ref, sk_ref, sv_ref, nq_ref, nk_ref, nv_ref):
    bf16 = jnp.bfloat16
    f32 = jnp.float32
    tokens = x_ref.shape[1]
    h = _rms_rows(x_ref[...], g1_ref[...]).astype(bf16)
    proj = jnp.dot(w_in_ref[...], h, preferred_element_type=f32)
    part = [proj[IN_OFFSETS[i]:IN_OFFSETS[i + 1]] for i in range(len(IN_SPLIT_WIDTHS))]
    a_cq, a_ckv, a_kr, b_q, b_k, b_v, c_q, c_k, c_v, d_q, d_k, d_v = part
    c32, s32 = c32_ref[...], s32_ref[...]
    c64, s64 = c64_ref[...], s64_ref[...]
    ones = _ones_rows(tokens)

    mla_scale = MLA_QK_DIM ** -0.5 * LOG2E
    cqn = _rms_rows(a_cq, qg_ref[...]).astype(bf16)
    q_all = jnp.dot(w_uq_ref[...], cqn, preferred_element_type=f32)
    lat = _rms_rows(a_ckv, kvg_ref[...]).astype(bf16)
    k_nope = jnp.dot(w_uk_ref[...], lat, preferred_element_type=f32)
    v_all = jnp.dot(w_uv_ref[...], lat, preferred_element_type=f32)
    half = MLA_ROPE_DIM // 2
    kp1, kp2 = _rope_rows(a_kr[:half], a_kr[half:], c32, s32)
    for hd in range(MLA_HEADS):
        q = q_all[hd * MLA_QK_DIM:(hd + 1) * MLA_QK_DIM]
        q1, q2 = _rope_rows(q[MLA_NOPE_DIM:MLA_NOPE_DIM + half], q[MLA_NOPE_DIM + half:], c32, s32)
        mq_ref[hd, 0:MLA_NOPE_DIM] = (q[:MLA_NOPE_DIM] * mla_scale).astype(bf16)
        mq_ref[hd, MLA_NOPE_DIM:MLA_NOPE_DIM + half] = (q1 * mla_scale).astype(bf16)
        mq_ref[hd, MLA_NOPE_DIM + half:MLA_QK_DIM] = (q2 * mla_scale).astype(bf16)
        mk_ref[hd, 0:MLA_NOPE_DIM] = k_nope[hd * MLA_NOPE_DIM:(hd + 1) * MLA_NOPE_DIM].astype(bf16)
        mk_ref[hd, MLA_NOPE_DIM:MLA_NOPE_DIM + half] = kp1.astype(bf16)
        mk_ref[hd, MLA_NOPE_DIM + half:MLA_QK_DIM] = kp2.astype(bf16)
        mv_ref[hd, 0:HEAD_V] = v_all[hd * HEAD_V:(hd + 1) * HEAD_V].astype(bf16)
        mv_ref[hd, HEAD_V:V_AUG] = ones

    diff_scale = DIFF_QK_DIM ** -0.5 * LOG2E
    half = DIFF_QK_DIM // 2
    for m in range(2 * DIFF_HEADS):
        lo = m * DIFF_QK_DIM
        q1, q2 = _rope_rows(b_q[lo:lo + half], b_q[lo + half:lo + DIFF_QK_DIM], c32, s32)
        k1, k2 = _rope_rows(b_k[lo:lo + half], b_k[lo + half:lo + DIFF_QK_DIM], c32, s32)
        dq_ref[m, 0:half] = (q1 * diff_scale).astype(bf16)
        dq_ref[m, half:DIFF_QK_DIM] = (q2 * diff_scale).astype(bf16)
        dk_ref[m, 0:half] = k1.astype(bf16)
        dk_ref[m, half:DIFF_QK_DIM] = k2.astype(bf16)
    for hd in range(DIFF_HEADS):
        dv_ref[hd, 0:HEAD_V] = b_v[hd * HEAD_V:(hd + 1) * HEAD_V].astype(bf16)
        dv_ref[hd, HEAD_V:V_AUG] = ones

    swa_scale = SWA_HEAD_DIM ** -0.5 * LOG2E
    half = SWA_HEAD_DIM // 2
    for hd in range(SWA_Q_HEADS):
        lo = hd * SWA_HEAD_DIM
        q1, q2 = _rope_rows(c_q[lo:lo + half], c_q[lo + half:lo + SWA_HEAD_DIM], c64, s64)
        sq_ref[hd, 0:half] = (q1 * swa_scale).astype(bf16)
        sq_ref[hd, half:SWA_HEAD_DIM] = (q2 * swa_scale).astype(bf16)
    for hd in range(SWA_KV_HEADS):
        lo = hd * SWA_HEAD_DIM
        k1, k2 = _rope_rows(c_k[lo:lo + half], c_k[lo + half:lo + SWA_HEAD_DIM], c64, s64)
        sk_ref[hd, 0:half] = k1.astype(bf16)
        sk_ref[hd, half:SWA_HEAD_DIM] = k2.astype(bf16)
        sv_ref[hd, 0:HEAD_V] = c_v[lo:lo + HEAD_V].astype(bf16)
        sv_ref[hd, HEAD_V:V_AUG] = ones

    na_scale = NA_HEAD_DIM ** -0.5 * LOG2E
    for hd in range(NA_HEADS):
        lo = hd * NA_HEAD_DIM
        nq_ref[hd] = (d_q[lo:lo + NA_HEAD_DIM] * na_scale).astype(bf16)
        nk_ref[hd] = d_k[lo:lo + NA_HEAD_DIM].astype(bf16)
        nv_ref[hd, 0:HEAD_V] = d_v[lo:lo + HEAD_V].astype(bf16)
        nv_ref[hd, HEAD_V:V_AUG] = ones


def _prep(xT, g1, w_inT, qg, w_uqT, kvg, w_ukT, w_uvT, c32, s32, c64, s64):
    S = xT.shape[1]
    tt = min(TOKEN_TILE, S)
    bf16 = jnp.bfloat16

    def tok(rows):
        return pl.BlockSpec((rows, tt), lambda i: (0, i))

    def head_out(heads, rows):
        return (jax.ShapeDtypeStruct((heads, rows, S), bf16),
                pl.BlockSpec((heads, rows, tt), lambda i: (0, 0, i)))

    outs = [head_out(MLA_HEADS, MLA_QK_DIM), head_out(MLA_HEADS, MLA_QK_DIM), head_out(MLA_HEADS, V_AUG),
            head_out(2 * DIFF_HEADS, DIFF_QK_DIM), head_out(2 * DIFF_HEADS, DIFF_QK_DIM),
            head_out(DIFF_HEADS, V_AUG),
            head_out(SWA_Q_HEADS, SWA_HEAD_DIM), head_out(SWA_KV_HEADS, SWA_HEAD_DIM),
            head_out(SWA_KV_HEADS, V_AUG),
            head_out(NA_HEADS, NA_HEAD_DIM), head_out(NA_HEADS, NA_HEAD_DIM), head_out(NA_HEADS, V_AUG)]
    return pl.pallas_call(
        _prep_kernel,
        grid=(S // tt,),
        in_specs=[tok(D_MODEL), _const_spec(g1.shape), _const_spec(w_inT.shape), _const_spec(qg.shape),
                  _const_spec(w_uqT.shape), _const_spec(kvg.shape), _const_spec(w_ukT.shape),
                  _const_spec(w_uvT.shape), tok(c32.shape[0]), tok(s32.shape[0]), tok(c64.shape[0]),
                  tok(s64.shape[0])],
        out_specs=[o[1] for o in outs],
        out_shape=[o[0] for o in outs],
        compiler_params=_params(1),
        name="prep",
    )(xT, g1, w_inT, qg, w_uqT, kvg, w_ukT, w_uvT, c32, s32, c64, s64)


def _dense_kernel(q_ref, qn_ref, k_ref, v_ref, o_ref, acc_ref, m_ref,
                  s0_ref, s1_ref, b0_ref, b1_ref, qq_ref):
    f32 = jnp.float32
    nk, tk = k_ref.shape[1], k_ref.shape[2]
    first_tile = pl.program_id(1) == 0
    qq_ref[0] = q_ref[0]
    qq_ref[1] = qn_ref[0]
    sub = min(DENSE_SUB, tk)
    subs = [slice(c * sub, (c + 1) * sub) for c in range(tk // sub)]
    bufs = ((s0_ref, b0_ref), (s1_ref, b1_ref))

    def region(q_a, kb_a, kb_c, parity, m_old):
        s_a, b_a = bufs[1 - parity]
        s_c, b_c = bufs[parity]
        if kb_c is not None:
            m_new = jnp.maximum(m_old, b_c[...])
            alpha = jnp.exp2(m_old - m_new)
            m_ref[...] = m_new
        bmax = pv = None
        for c in subs:
            s = jnp.dot(k_ref[0, kb_a, c, :], q_a, preferred_element_type=f32)
            s_a[c, :] = s
            mc = jnp.max(s, axis=0, keepdims=True)
            bmax = mc if bmax is None else jnp.maximum(bmax, mc)
            if kb_c is not None:
                p = jnp.exp2(s_c[c, :] - m_new).astype(jnp.bfloat16)
                t = jnp.dot(v_ref[0, kb_c, :, c], p, preferred_element_type=f32)
                pv = t if pv is None else pv + t
        b_a[...] = bmax
        if kb_c is not None:
            acc_ref[...] = alpha * acc_ref[...] + pv

    @pl.when(first_tile)
    def _():
        acc_ref[...] = jnp.zeros_like(acc_ref)
        region(q_ref[0], 0, None, 1, None)

    def block(t, carry):
        last = t == nk - 1
        for parity in (0, 1):
            @pl.when(t % 2 == parity)
            def _():
                q_a = qq_ref[jnp.where(last, 1, 0)]
                kb_a = jnp.where(last, 0, t + 1)
                m_old = jnp.where(t == 0, NEG_INF, m_ref[...])
                region(q_a, kb_a, t, parity, m_old)
        return carry

    jax.lax.fori_loop(0, nk, block, 0)
    acc = acc_ref[...]
    o_ref[0] = acc[0:HEAD_V] / acc[HEAD_V:HEAD_V + 1]
    acc_ref[...] = jnp.zeros_like(acc_ref)


def _dense_attention(qT, kT, vT, maps_per_value, name):
    M, d, S = qT.shape
    tq = min(DENSE_TQ, S)
    tk = min(DENSE_TK, S)
    nk, nq = S // tk, S // tq
    assert nk >= 2 and nk % 2 == 0
    k = jnp.swapaxes(kT, 1, 2).reshape(M, nk, tk, d)
    v = jnp.swapaxes(vT.reshape(vT.shape[0], V_AUG, nk, tk), 1, 2)
    row = pltpu.VMEM((1, tq), jnp.float32)
    return pl.pallas_call(
        _dense_kernel,
        grid=(M, nq),
        in_specs=[pl.BlockSpec((1, d, tq), lambda m, i: (m, 0, i)),
                  pl.BlockSpec((1, d, tq), lambda m, i: (m, 0, jnp.minimum(i + 1, nq - 1))),
                  pl.BlockSpec((1, nk, tk, d), lambda m, i: (m, 0, 0, 0)),
                  pl.BlockSpec((1, nk, V_AUG, tk), lambda m, i: (m // maps_per_value, 0, 0, 0))],
        out_specs=pl.BlockSpec((1, HEAD_V, tq), lambda m, i: (m, 0, i)),
        out_shape=jax.ShapeDtypeStruct((M, HEAD_V, S), jnp.float32),
        scratch_shapes=[pltpu.VMEM((V_AUG, tq), jnp.float32), row,
                        pltpu.VMEM((tk, tq), jnp.float32), pltpu.VMEM((tk, tq), jnp.float32),
                        row, row, pltpu.VMEM((2, d, tq), qT.dtype)],
        compiler_params=_params(2),
        name=name,
    )(qT, qT, k, v)


def _local_kernel(sink_ref, *refs, groups):
    f32 = jnp.float32
    n_in, n_g = 8, len(groups)
    outs = refs[n_in * n_g:(n_in + 1) * n_g]
    scratch = refs[(n_in + 1) * n_g:]
    j = pl.program_id(0)
    n_heads = max(refs[n_in * g].shape[0] for g in range(n_g))

    @pl.when(j == 0)
    def _():
        for g in range(n_g):
            scratch[4 * g + 1][...] = jnp.zeros_like(scratch[4 * g + 1])
            scratch[4 * g + 3][...] = jnp.zeros_like(scratch[4 * g + 3])

    def step(parity):
        for h in range(n_heads):
            for g, (has_sink, q_per_kv) in enumerate(groups):
                q_ref, k0_ref, k1_ref, k2_ref, v0_ref, v1_ref, v2_ref, b_ref = refs[n_in * g:n_in * (g + 1)]
                if h >= q_ref.shape[0]:
                    continue
                s_a, s_b = scratch[4 * g + parity], scratch[4 * g + 1 - parity]
                m_a, m_b = scratch[4 * g + 2 + parity], scratch[4 * g + 3 - parity]
                blk = q_ref.shape[2]
                chunks = [slice(c * blk, (c + 1) * blk) for c in range(LOCAL_CHUNKS)]
                kv = h // q_per_kv
                q = q_ref[h]
                m = None
                for c, k in zip(chunks, (k0_ref, k1_ref, k2_ref)):
                    s = jnp.dot(k[kv], q, preferred_element_type=f32) + b_ref[0, h % b_ref.shape[1], c]
                    s_a[h, c] = s
                    mc = jnp.max(s, axis=0, keepdims=True)
                    m = mc if m is None else jnp.maximum(m, mc)
                if has_sink:
                    m = jnp.maximum(m, sink_ref[h])
                m_a[h] = m
                m = m_b[h]
                pv = None
                for c, v in zip(chunks, (v0_ref, v1_ref, v2_ref)):
                    p = jnp.exp2(s_b[h, c] - m).astype(jnp.bfloat16)
                    t = jnp.dot(v[kv], p, preferred_element_type=f32)
                    pv = t if pv is None else pv + t
                l = pv[HEAD_V:HEAD_V + 1]
                if has_sink:
                    l = l + jnp.exp2(sink_ref[h] - m)
                outs[g][h] = pv[0:HEAD_V] / l

    for parity in (0, 1):
        @pl.when(j % 2 == parity)
        def _():
            step(parity)


def _local_attention(sinks, group_args):
    S = group_args[0][0].shape[2]
    blk = LOCAL_BLK
    nb = S // blk
    assert nb >= LOCAL_CHUNKS

    def base(i):
        return jnp.clip(i - 1, 0, nb - LOCAL_CHUNKS)

    def case(i):
        return jnp.where(i == 0, 0, jnp.where(i == nb - 1, 2, 1))

    def scored(j):
        return jnp.minimum(j, nb - 1)

    def finished(j):
        return jnp.maximum(j - 1, 0)

    operands, in_specs, out_specs, out_shapes, groups, scratch = (
        [sinks], [pl.BlockSpec(memory_space=pltpu.SMEM)], [], [], [], [])
    for qT, kT, vT, bias, q_per_kv, has_sink in group_args:
        Hq, d, _ = qT.shape
        Hk, Hb = kT.shape[0], bias.shape[1]
        k = jnp.swapaxes(kT, 1, 2)
        operands += [qT, k, k, k, vT, vT, vT, bias]
        in_specs.append(pl.BlockSpec((Hq, d, blk), lambda j: (0, 0, scored(j))))
        in_specs += [pl.BlockSpec((Hk, blk, d),
                                  functools.partial(lambda c, j: (0, base(scored(j)) + c, 0), c))
                     for c in range(LOCAL_CHUNKS)]
        in_specs += [pl.BlockSpec((Hk, V_AUG, blk),
                                  functools.partial(lambda c, j: (0, 0, base(finished(j)) + c), c))
                     for c in range(LOCAL_CHUNKS)]
        in_specs.append(pl.BlockSpec((1, Hb, LOCAL_CHUNKS * blk, blk),
                                     lambda j: (case(scored(j)), 0, 0, 0)))
        out_specs.append(pl.BlockSpec((Hq, HEAD_V, blk), lambda j: (0, 0, finished(j))))
        out_shapes.append(jax.ShapeDtypeStruct((Hq, HEAD_V, S), jnp.float32))
        groups.append((has_sink, q_per_kv))
        scratch += [pltpu.VMEM((Hq, LOCAL_CHUNKS * blk, blk), jnp.float32)] * 2
        scratch += [pltpu.VMEM((Hq, 1, blk), jnp.float32)] * 2
    return pl.pallas_call(
        functools.partial(_local_kernel, groups=tuple(groups)),
        grid=(nb + 1,),
        in_specs=in_specs,
        out_specs=out_specs,
        out_shape=out_shapes,
        scratch_shapes=scratch,
        compiler_params=_params(1),
        name="local",
    )(*operands)


def _swa_bias(nb):
    blk = LOCAL_BLK
    kk = np.arange(LOCAL_CHUNKS * blk)[:, None]
    qq = np.arange(blk)[None, :]
    out = []
    for q_off in (0, blk, 2 * blk):
        valid = np.abs(kk - (q_off + qq)) <= WINDOW
        out.append(np.where(valid, 0.0, NEG_INF))
    return jnp.asarray(np.stack(out)[:, None], jnp.float32)


def _na_bias(rpb, S):
    blk = LOCAL_BLK
    rows = S // GRID_W
    rpq = blk // GRID_W
    kpq = LOCAL_CHUNKS * rpq
    nb = S // blk
    n_off = 2 * NA_ROWS - 1
    kc = np.arange(GRID_W)[:, None]
    w = np.arange(GRID_W)[None, :]
    cs = np.clip(w - NA_COLS // 2, 0, GRID_W - NA_COLS)
    col_ok = (kc >= cs) & (kc < cs + NA_COLS)
    onehot = (kc - w + NA_COLS - 1)[None] == np.arange(2 * NA_COLS - 1)[:, None, None]
    onehot = jnp.asarray(onehot.reshape(2 * NA_COLS - 1, GRID_W * GRID_W), jnp.float32)
    block_of = np.full((3, kpq, rpq), n_off, np.int32)
    for c, i in enumerate((0, 1, nb - 1)):
        base_row = rpq * int(np.clip(i - 1, 0, nb - LOCAL_CHUNKS))
        for ki in range(kpq):
            for qj in range(rpq):
                r, kr = rpq * i + qj, base_row + ki
                rs = int(np.clip(r - NA_ROWS // 2, 0, rows - NA_ROWS))
                if rs <= kr < rs + NA_ROWS:
                    block_of[c, ki, qj] = kr - r + NA_ROWS - 1
    L, H = rpb.shape[:2]
    toe = jnp.einsum("lhab,bn->lhan", rpb, onehot, precision=jax.lax.Precision.HIGHEST)
    toe = toe.reshape(L, H, n_off, GRID_W, GRID_W) * LOG2E
    toe = jnp.where(jnp.asarray(col_ok), toe, NEG_INF)
    toe = jnp.concatenate([toe, jnp.full((L, H, 1, GRID_W, GRID_W), NEG_INF, jnp.float32)], axis=2)
    t = jnp.take(toe, jnp.asarray(block_of.reshape(-1)), axis=2)
    t = t.reshape(L, H, 3, kpq, rpq, GRID_W, GRID_W)
    t = jnp.transpose(t, (0, 2, 1, 3, 5, 4, 6))
    return t.reshape(L, 3, H, kpq * GRID_W, rpq * GRID_W)


def _post_kernel(x_ref, oa_ref, ob_ref, oc_ref, od_ref, ga_ref, gc_ref, gd_ref, gsub_ref,
                 lq1_ref, lk1_ref, lq2_ref, lk2_ref, linit_ref, w_out_ref, g2_ref, w_up_ref,
                 w_down_ref, gf_ref, y_ref, *, final):
    bf16 = jnp.bfloat16
    f32 = jnp.float32
    tokens = x_ref.shape[1]

    def heads(ref):
        return ref[...].reshape(ref.shape[0] * ref.shape[1], tokens)

    ya = _rms_rows(heads(oa_ref), ga_ref[...])
    yc = _rms_rows(heads(oc_ref), gc_ref[...])
    yd = _rms_rows(heads(od_ref), gd_ref[...])
    lam_init = linit_ref[...]
    lam = (jnp.exp(jnp.sum(lq1_ref[...] * lk1_ref[...], axis=1, keepdims=True))
           - jnp.exp(jnp.sum(lq2_ref[...] * lk2_ref[...], axis=1, keepdims=True)) + lam_init)
    yb = []
    for hd in range(DIFF_HEADS):
        w = ob_ref[2 * hd] - lam * ob_ref[2 * hd + 1]
        yb.append(_rms_rows(w, gsub_ref[...]) * (1.0 - lam_init))
    mix = jnp.concatenate([ya] + yb + [yc, yd], axis=0).astype(bf16)
    x1 = x_ref[...] + jnp.dot(w_out_ref[...], mix, preferred_element_type=f32)
    h2 = _rms_rows(x1, g2_ref[...]).astype(bf16)
    u = jnp.dot(w_up_ref[...], h2, preferred_element_type=f32)
    a = jnp.square(jnp.maximum(u, 0.0)).astype(bf16)
    x2 = x1 + jnp.dot(w_down_ref[...], a, preferred_element_type=f32)
    if final:
        x2 = _rms_rows(x2, gf_ref[...])
    y_ref[...] = x2


def _post(xT, oa, ob, oc, od, ga, gc, gd, gsub, lq1, lk1, lq2, lk2, linit, w_outT, g2, w_upT,
          w_downT, gf, final):
    S = xT.shape[1]
    tt = min(TOKEN_TILE, S)

    def tok(rows):
        return pl.BlockSpec((rows, tt), lambda i: (0, i))

    def heads(arr):
        return pl.BlockSpec((arr.shape[0], arr.shape[1], tt), lambda i: (0, 0, i))

    def resident(arr):
        nd = arr.ndim
        return pl.BlockSpec(arr.shape, lambda *_: (0,) * nd, pipeline_mode=pl.Buffered(1))

    small = [ga, gc, gd, gsub, lq1, lk1, lq2, lk2, linit]
    return pl.pallas_call(
        functools.partial(_post_kernel, final=final),
        grid=(S // tt,),
        in_specs=[tok(D_MODEL), heads(oa), heads(ob), heads(oc), heads(od)]
                 + [_const_spec(a.shape) for a in small]
                 + [resident(w_outT), _const_spec(g2.shape), resident(w_upT), resident(w_downT),
                    _const_spec(gf.shape)],
        out_specs=tok(D_MODEL),
        out_shape=jax.ShapeDtypeStruct((D_MODEL, S), jnp.float32),
        compiler_params=_params(1),
        name="post_final" if final else "post",
    )(xT, oa, ob, oc, od, *small, w_outT, g2, w_upT, w_downT, gf)


def _rope_tables_T(S, dim):
    inv = 1.0 / (ROPE_THETA ** (jnp.arange(0, dim, 2, dtype=jnp.float32) / dim))
    ang = inv[:, None] * jnp.arange(S, dtype=jnp.float32)[None, :]
    return jnp.cos(ang), jnp.sin(ang)


def _col(v):
    return v.astype(jnp.float32)[:, None]


def _wT(w):
    return jnp.swapaxes(w, -1, -2).astype(jnp.bfloat16)


@jax.jit
def _forward(x, norm1_g, w_in, mla_q_norm_g, mla_w_uq, mla_kv_norm_g, mla_w_uk, mla_w_uv,
             diff_lambda_q1, diff_lambda_k1, diff_lambda_q2, diff_lambda_k2, diff_subln_g,
             swa_sinks, na_rpb, out_g_mla, out_g_swa, out_g_na, w_out, norm2_g, w_up, w_down,
             final_norm_g):
    B, S, D = x.shape
    assert B == 1 and D == D_MODEL and S % TOKEN_TILE == 0 and S % GRID_W == 0
    c32, s32 = _rope_tables_T(S, MLA_ROPE_DIM)
    c64, s64 = _rope_tables_T(S, SWA_HEAD_DIM)
    swa_bias = _swa_bias(S // LOCAL_BLK)
    na_bias = _na_bias(na_rpb.astype(jnp.float32), S)
    xT = x[0].T
    gf = _col(final_norm_g)
    for l in range(DEPTH):
        prep = _prep(xT, _col(norm1_g[l]), _wT(w_in[l]), _col(mla_q_norm_g[l]), _wT(mla_w_uq[l]),
                     _col(mla_kv_norm_g[l]), _wT(mla_w_uk[l]), _wT(mla_w_uv[l]), c32, s32, c64, s64)
        mq, mk, mv, dq, dk, dv, sq, sk, sv, nq, nk, nv = prep
        oa = _dense_attention(mq, mk, mv, 1, "dense_mla")
        ob = _dense_attention(dq, dk, dv, 2, "dense_diff")
        oc, od = _local_attention(swa_sinks[l].astype(jnp.float32) * LOG2E,
                                  [(sq, sk, sv, swa_bias, SWA_GROUP, True),
                                   (nq, nk, nv, na_bias[l], 1, False)])
        lam_init = 0.8 - 0.6 * math.exp(-0.3 * l)
        row = lambda v: v.astype(jnp.float32)[None, :]
        xT = _post(xT, oa, ob, oc, od, _col(out_g_mla[l]), _col(out_g_swa[l]), _col(out_g_na[l]),
                   _col(diff_subln_g[l]), row(diff_lambda_q1[l]), row(diff_lambda_k1[l]),
                   row(diff_lambda_q2[l]), row(diff_lambda_k2[l]),
                   jnp.full((1, 1), lam_init, jnp.float32), _wT(w_out[l]), _col(norm2_g[l]),
                   _wT(w_up[l]), _wT(w_down[l]), gf, final=(l == DEPTH - 1))
    return xT.T[None]


def kernel(x, norm1_g, w_in, mla_q_norm_g, mla_w_uq, mla_kv_norm_g, mla_w_uk, mla_w_uv,
           diff_lambda_q1, diff_lambda_k1, diff_lambda_q2, diff_lambda_k2, diff_subln_g,
           swa_sinks, na_rpb, out_g_mla, out_g_swa, out_g_na, w_out, norm2_g, w_up, w_down,
           final_norm_g):
    return _forward(x, norm1_g, w_in, mla_q_norm_g, mla_w_uq, mla_kv_norm_g, mla_w_uk, mla_w_uv,
                    diff_lambda_q1, diff_lambda_k1, diff_lambda_q2, diff_lambda_k2, diff_subln_g,
                    swa_sinks, na_rpb, out_g_mla, out_g_swa, out_g_na, w_out, norm2_g, w_up,
                    w_down, final_norm_g)
```

```python
import functools
import math

import jax
import jax.numpy as jnp
import numpy as np
from jax.experimental import pallas as pl
from jax.experimental.pallas import tpu as pltpu

D_MODEL = 1024
DEPTH = 4
GRID_W = 64
ROPE_THETA = 10000.0
EPS = 1e-6
NEG_INF = -1e30

MLA_HEADS = 4
MLA_Q_RANK = 192
MLA_KV_RANK = 128
MLA_NOPE_DIM = 64
MLA_ROPE_DIM = 32
MLA_V_DIM = 64
MLA_QK_DIM = MLA_NOPE_DIM + MLA_ROPE_DIM
DIFF_HEADS = 4
DIFF_QK_DIM = 32
DIFF_V_DIM = 64
SWA_Q_HEADS = 4
SWA_KV_HEADS = 2
SWA_GROUP = SWA_Q_HEADS // SWA_KV_HEADS
SWA_HEAD_DIM = 64
WINDOW = 128
NA_HEADS = 4
NA_HEAD_DIM = 64
NA_ROWS = 8
NA_COLS = 16
D_FF = 4 * D_MODEL
HEAD_V = 64
GROUP_OUT = 256

IN_SPLIT_WIDTHS = (
    MLA_Q_RANK, MLA_KV_RANK, MLA_ROPE_DIM,
    DIFF_HEADS * 2 * DIFF_QK_DIM, DIFF_HEADS * 2 * DIFF_QK_DIM, DIFF_HEADS * DIFF_V_DIM,
    SWA_Q_HEADS * SWA_HEAD_DIM, SWA_KV_HEADS * SWA_HEAD_DIM, SWA_KV_HEADS * SWA_HEAD_DIM,
    NA_HEADS * NA_HEAD_DIM, NA_HEADS * NA_HEAD_DIM, NA_HEADS * NA_HEAD_DIM,
)
IN_COLS = sum(IN_SPLIT_WIDTHS)
IN_OFFSETS = tuple(int(v) for v in np.cumsum((0,) + IN_SPLIT_WIDTHS))

V7X_LANES = 128
V7X_BF16_SUBLANES = 16
V7X_VMEM_LIMIT_BYTES = 56 * 1024 * 1024

LOG2E = math.log2(math.e)
V_AUG = HEAD_V + V7X_BF16_SUBLANES

TOKEN_TILE = 512
DENSE_TQ = 512
DENSE_TILES_PER_STEP = 2
DENSE_TK = 8192
DENSE_SUB = 256
LOCAL_BLK = 256
LOCAL_CHUNKS = 3


def _params(n_axes):
    return pltpu.CompilerParams(
        dimension_semantics=("arbitrary",) * n_axes,
        vmem_limit_bytes=V7X_VMEM_LIMIT_BYTES,
    )


def _const_spec(shape):
    nd = len(shape)
    return pl.BlockSpec(shape, lambda *_: (0,) * nd)


def _rms_rows(x, g):
    r = jax.lax.rsqrt(jnp.mean(x * x, axis=0, keepdims=True) + EPS)
    return (x * r) * g


def _rope_rows(x1, x2, c, s):
    return x1 * c - x2 * s, x2 * c + x1 * s


def _ones_rows(tokens):
    row = jax.lax.broadcasted_iota(jnp.int32, (V7X_BF16_SUBLANES, tokens), 0)
    return jnp.where(row == 0, 1.0, 0.0).astype(jnp.bfloat16)


def _prep_kernel(x_ref, g1_ref, w_in_ref, qg_ref, w_uq_ref, kvg_ref, w_uk_ref, w_uv_ref,
                 c32_ref, s32_ref, c64_ref, s64_ref,
                 mq_ref, mk_ref, mv_ref, dq_ref, dk_ref, dv_ref,
                 sq_ref, sk_ref, sv_ref, nq_ref, nk_ref, nv_ref):
    bf16 = jnp.bfloat16
    f32 = jnp.float32
    tokens = x_ref.shape[1]
    h = _rms_rows(x_ref[...], g1_ref[...]).astype(bf16)
    proj = jnp.dot(w_in_ref[...], h, preferred_element_type=f32)
    part = [proj[IN_OFFSETS[i]:IN_OFFSETS[i + 1]] for i in range(len(IN_SPLIT_WIDTHS))]
    a_cq, a_ckv, a_kr, b_q, b_k, b_v, c_q, c_k, c_v, d_q, d_k, d_v = part
    c32, s32 = c32_ref[...], s32_ref[...]
    c64, s64 = c64_ref[...], s64_ref[...]
    ones = _ones_rows(tokens)

    mla_scale = MLA_QK_DIM ** -0.5 * LOG2E
    cqn = _rms_rows(a_cq, qg_ref[...]).astype(bf16)
    q_all = jnp.dot(w_uq_ref[...], cqn, preferred_element_type=f32)
    lat = _rms_rows(a_ckv, kvg_ref[...]).astype(bf16)
    k_nope = jnp.dot(w_uk_ref[...], lat, preferred_element_type=f32)
    v_all = jnp.dot(w_uv_ref[...], lat, preferred_element_type=f32)
    half = MLA_ROPE_DIM // 2
    kp1, kp2 = _rope_rows(a_kr[:half], a_kr[half:], c32, s32)
    for hd in range(MLA_HEADS):
        q = q_all[hd * MLA_QK_DIM:(hd + 1) * MLA_QK_DIM]
        q1, q2 = _rope_rows(q[MLA_NOPE_DIM:MLA_NOPE_DIM + half], q[MLA_NOPE_DIM + half:], c32, s32)
        mq_ref[hd, 0:MLA_NOPE_DIM] = (q[:MLA_NOPE_DIM] * mla_scale).astype(bf16)
        mq_ref[hd, MLA_NOPE_DIM:MLA_NOPE_DIM + half] = (q1 * mla_scale).astype(bf16)
        mq_ref[hd, MLA_NOPE_DIM + half:MLA_QK_DIM] = (q2 * mla_scale).astype(bf16)
        mk_ref[hd, 0:MLA_NOPE_DIM] = k_nope[hd * MLA_NOPE_DIM:(hd + 1) * MLA_NOPE_DIM].astype(bf16)
        mk_ref[hd, MLA_NOPE_DIM:MLA_NOPE_DIM + half] = kp1.astype(bf16)
        mk_ref[hd, MLA_NOPE_DIM + half:MLA_QK_DIM] = kp2.astype(bf16)
        mv_ref[hd, 0:HEAD_V] = v_all[hd * HEAD_V:(hd + 1) * HEAD_V].astype(bf16)
        mv_ref[hd, HEAD_V:V_AUG] = ones

    diff_scale = DIFF_QK_DIM ** -0.5 * LOG2E
    half = DIFF_QK_DIM // 2
    for m in range(2 * DIFF_HEADS):
        lo = m * DIFF_QK_DIM
        q1, q2 = _rope_rows(b_q[lo:lo + half], b_q[lo + half:lo + DIFF_QK_DIM], c32, s32)
        k1, k2 = _rope_rows(b_k[lo:lo + half], b_k[lo + half:lo + DIFF_QK_DIM], c32, s32)
        dq_ref[m, 0:half] = (q1 * diff_scale).astype(bf16)
        dq_ref[m, half:DIFF_QK_DIM] = (q2 * diff_scale).astype(bf16)
        dk_ref[m, 0:half] = k1.astype(bf16)
        dk_ref[m, half:DIFF_QK_DIM] = k2.astype(bf16)
    for hd in range(DIFF_HEADS):
        dv_ref[hd, 0:HEAD_V] = b_v[hd * HEAD_V:(hd + 1) * HEAD_V].astype(bf16)
        dv_ref[hd, HEAD_V:V_AUG] = ones

    swa_scale = SWA_HEAD_DIM ** -0.5 * LOG2E
    half = SWA_HEAD_DIM // 2
    for hd in range(SWA_Q_HEADS):
        lo = hd * SWA_HEAD_DIM
        q1, q2 = _rope_rows(c_q[lo:lo + half], c_q[lo + half:lo + SWA_HEAD_DIM], c64, s64)
        sq_ref[hd, 0:half] = (q1 * swa_scale).astype(bf16)
        sq_ref[hd, half:SWA_HEAD_DIM] = (q2 * swa_scale).astype(bf16)
    for hd in range(SWA_KV_HEADS):
        lo = hd * SWA_HEAD_DIM
        k1, k2 = _rope_rows(c_k[lo:lo + half], c_k[lo + half:lo + SWA_HEAD_DIM], c64, s64)
        sk_ref[hd, 0:half] = k1.astype(bf16)
        sk_ref[hd, half:SWA_HEAD_DIM] = k2.astype(bf16)
        sv_ref[hd, 0:HEAD_V] = c_v[lo:lo + HEAD_V].astype(bf16)
        sv_ref[hd, HEAD_V:V_AUG] = ones

    na_scale = NA_HEAD_DIM ** -0.5 * LOG2E
    for hd in range(NA_HEADS):
        lo = hd * NA_HEAD_DIM
        nq_ref[hd] = (d_q[lo:lo + NA_HEAD_DIM] * na_scale).astype(bf16)
        nk_ref[hd] = d_k[lo:lo + NA_HEAD_DIM].astype(bf16)
        nv_ref[hd, 0:HEAD_V] = d_v[lo:lo + HEAD_V].astype(bf16)
        nv_ref[hd, HEAD_V:V_AUG] = ones


def _prep(xT, g1, w_inT, qg, w_uqT, kvg, w_ukT, w_uvT, c32, s32, c64, s64):
    S = xT.shape[1]
    tt = min(TOKEN_TILE, S)
    bf16 = jnp.bfloat16

    def tok(rows):
        return pl.BlockSpec((rows, tt), lambda i: (0, i))

    def head_out(heads, rows):
        return (jax.ShapeDtypeStruct((heads, rows, S), bf16),
                pl.BlockSpec((heads, rows, tt), lambda i: (0, 0, i)))

    outs = [head_out(MLA_HEADS, MLA_QK_DIM), head_out(MLA_HEADS, MLA_QK_DIM), head_out(MLA_HEADS, V_AUG),
            head_out(2 * DIFF_HEADS, DIFF_QK_DIM), head_out(2 * DIFF_HEADS, DIFF_QK_DIM),
            head_out(DIFF_HEADS, V_AUG),
            head_out(SWA_Q_HEADS, SWA_HEAD_DIM), head_out(SWA_KV_HEADS, SWA_HEAD_DIM),
            head_out(SWA_KV_HEADS, V_AUG),
            head_out(NA_HEADS, NA_HEAD_DIM), head_out(NA_HEADS, NA_HEAD_DIM), head_out(NA_HEADS, V_AUG)]
    return pl.pallas_call(
        _prep_kernel,
        grid=(S // tt,),
        in_specs=[tok(D_MODEL), _const_spec(g1.shape), _const_spec(w_inT.shape), _const_spec(qg.shape),
                  _const_spec(w_uqT.shape), _const_spec(kvg.shape), _const_spec(w_ukT.shape),
                  _const_spec(w_uvT.shape), tok(c32.shape[0]), tok(s32.shape[0]), tok(c64.shape[0]),
                  tok(s64.shape[0])],
        out_specs=[o[1] for o in outs],
        out_shape=[o[0] for o in outs],
        compiler_params=_params(1),
        name="prep",
    )(xT, g1, w_inT, qg, w_uqT, kvg, w_ukT, w_uvT, c32, s32, c64, s64)


def _dense_kernel(q_ref, qn_ref, k_ref, v_ref, o_ref, acc_ref, m_ref,
                  s0_ref, s1_ref, b0_ref, b1_ref, qq_ref):
    f32 = jnp.float32
    nk, tk = k_ref.shape[1], k_ref.shape[2]
    tq = qn_ref.shape[2]
    n_tiles = q_ref.shape[2] // tq
    first_tile = pl.program_id(1) == 0
    for g in range(n_tiles):
        qq_ref[g] = q_ref[0, :, g * tq:(g + 1) * tq]
    qq_ref[n_tiles] = qn_ref[0]
    sub = min(DENSE_SUB, tk)
    subs = [slice(c * sub, (c + 1) * sub) for c in range(tk // sub)]
    bufs = ((s0_ref, b0_ref), (s1_ref, b1_ref))

    def region(q_a, kb_a, kb_c, parity, m_old):
        s_a, b_a = bufs[1 - parity]
        s_c, b_c = bufs[parity]
        if kb_c is not None:
            m_new = jnp.maximum(m_old, b_c[...])
            alpha = jnp.exp2(m_old - m_new)
            m_ref[...] = m_new
        bmax = pv = None
        for c in subs:
            s = jnp.dot(k_ref[0, kb_a, c, :], q_a, preferred_element_type=f32)
            s_a[c, :] = s
            mc = jnp.max(s, axis=0, keepdims=True)
            bmax = mc if bmax is None else jnp.maximum(bmax, mc)
            if kb_c is not None:
                p = jnp.exp2(s_c[c, :] - m_new).astype(jnp.bfloat16)
                t = jnp.dot(v_ref[0, kb_c, :, c], p, preferred_element_type=f32)
                pv = t if pv is None else pv + t
        b_a[...] = bmax
        if kb_c is not None:
            acc_ref[...] = alpha * acc_ref[...] + pv

    @pl.when(first_tile)
    def _():
        acc_ref[...] = jnp.zeros_like(acc_ref)
        m_ref[...] = jnp.full_like(m_ref, NEG_INF)
        region(qq_ref[0], 0, None, 1, None)

    def block(t, carry):
        tile = jax.lax.div(t, nk)
        kb = t - tile * nk
        last = kb == nk - 1
        for parity in (0, 1):
            @pl.when(t % 2 == parity)
            def _():
                q_a = qq_ref[tile + jnp.where(last, 1, 0)]
                kb_a = jnp.where(last, 0, kb + 1)
                m_old = jnp.where(kb == 0, NEG_INF, m_ref[...])
                region(q_a, kb_a, kb, parity, m_old)

        @pl.when(last)
        def _():
            acc = acc_ref[...]
            o_ref[0, :, pl.ds(pl.multiple_of(tile * tq, tq), tq)] = (
                acc[0:HEAD_V] / acc[HEAD_V:HEAD_V + 1])
            acc_ref[...] = jnp.zeros_like(acc_ref)
        return carry

    jax.lax.fori_loop(0, n_tiles * nk, block, 0)


def _dense_attention(qT, kT, vT, maps_per_value, name):
    M, d, S = qT.shape
    tq = min(DENSE_TQ, S)
    tk = min(DENSE_TK, S)
    nk, nq = S // tk, S // tq
    tiles = min(DENSE_TILES_PER_STEP, nq)
    assert nk >= 2 and nk % 2 == 0 and nq % tiles == 0
    k = jnp.swapaxes(kT, 1, 2).reshape(M, nk, tk, d)
    v = jnp.swapaxes(vT.reshape(vT.shape[0], V_AUG, nk, tk), 1, 2)
    row = pltpu.VMEM((1, tq), jnp.float32)
    return pl.pallas_call(
        _dense_kernel,
        grid=(M, nq // tiles),
        in_specs=[pl.BlockSpec((1, d, tiles * tq), lambda m, i: (m, 0, i)),
                  pl.BlockSpec((1, d, tq), lambda m, i: (m, 0, jnp.minimum(tiles * (i + 1), nq - 1))),
                  pl.BlockSpec((1, nk, tk, d), lambda m, i: (m, 0, 0, 0)),
                  pl.BlockSpec((1, nk, V_AUG, tk), lambda m, i: (m // maps_per_value, 0, 0, 0))],
        out_specs=pl.BlockSpec((1, HEAD_V, tiles * tq), lambda m, i: (m, 0, i)),
        out_shape=jax.ShapeDtypeStruct((M, HEAD_V, S), jnp.float32),
        scratch_shapes=[pltpu.VMEM((V_AUG, tq), jnp.float32), row,
                        pltpu.VMEM((tk, tq), jnp.float32), pltpu.VMEM((tk, tq), jnp.float32),
                        row, row, pltpu.VMEM((tiles + 1, d, tq), qT.dtype)],
        compiler_params=_params(2),
        name=name,
    )(qT, qT, k, v)


def _local_kernel(sink_ref, *refs, groups):
    f32 = jnp.float32
    n_in, n_g = 8, len(groups)
    outs = refs[n_in * n_g:(n_in + 1) * n_g]
    scratch = refs[(n_in + 1) * n_g:]
    j = pl.program_id(0)
    n_heads = max(refs[n_in * g].shape[0] for g in range(n_g))

    @pl.when(j == 0)
    def _():
        for g in range(n_g):
            scratch[4 * g + 1][...] = jnp.zeros_like(scratch[4 * g + 1])
            scratch[4 * g + 3][...] = jnp.zeros_like(scratch[4 * g + 3])

    def step(parity):
        for h in range(n_heads):
            for g, (has_sink, q_per_kv) in enumerate(groups):
                q_ref, k0_ref, k1_ref, k2_ref, v0_ref, v1_ref, v2_ref, b_ref = refs[n_in * g:n_in * (g + 1)]
                if h >= q_ref.shape[0]:
                    continue
                s_a, s_b = scratch[4 * g + parity], scratch[4 * g + 1 - parity]
                m_a, m_b = scratch[4 * g + 2 + parity], scratch[4 * g + 3 - parity]
                blk = q_ref.shape[2]
                chunks = [slice(c * blk, (c + 1) * blk) for c in range(LOCAL_CHUNKS)]
                kv = h // q_per_kv
                q = q_ref[h]
                m = None
                for c, k in zip(chunks, (k0_ref, k1_ref, k2_ref)):
                    s = jnp.dot(k[kv], q, preferred_element_type=f32) + b_ref[0, h % b_ref.shape[1], c]
                    s_a[h, c] = s
                    mc = jnp.max(s, axis=0, keepdims=True)
                    m = mc if m is None else jnp.maximum(m, mc)
                if has_sink:
                    m = jnp.maximum(m, sink_ref[h])
                m_a[h] = m
                m = m_b[h]
                pv = None
                for c, v in zip(chunks, (v0_ref, v1_ref, v2_ref)):
                    p = jnp.exp2(s_b[h, c] - m).astype(jnp.bfloat16)
                    t = jnp.dot(v[kv], p, preferred_element_type=f32)
                    pv = t if pv is None else pv + t
                l = pv[HEAD_V:HEAD_V + 1]
                if has_sink:
                    l = l + jnp.exp2(sink_ref[h] - m)
                outs[g][h] = pv[0:HEAD_V] / l

    for parity in (0, 1):
        @pl.when(j % 2 == parity)
        def _():
            step(parity)


def _local_attention(sinks, group_args):
    S = group_args[0][0].shape[2]
    blk = LOCAL_BLK
    nb = S // blk
    assert nb >= LOCAL_CHUNKS

    def base(i):
        return jnp.clip(i - 1, 0, nb - LOCAL_CHUNKS)

    def case(i):
        return jnp.where(i == 0, 0, jnp.where(i == nb - 1, 2, 1))

    def scored(j):
        return jnp.minimum(j, nb - 1)

    def finished(j):
        return jnp.maximum(j - 1, 0)

    operands, in_specs, out_specs, out_shapes, groups, scratch = (
        [sinks], [pl.BlockSpec(memory_space=pltpu.SMEM)], [], [], [], [])
    for qT, kT, vT, bias, q_per_kv, has_sink in group_args:
        Hq, d, _ = qT.shape
        Hk, Hb = kT.shape[0], bias.shape[1]
        k = jnp.swapaxes(kT, 1, 2)
        operands += [qT, k, k, k, vT, vT, vT, bias]
        in_specs.append(pl.BlockSpec((Hq, d, blk), lambda j: (0, 0, scored(j))))
        in_specs += [pl.BlockSpec((Hk, blk, d),
                                  functools.partial(lambda c, j: (0, base(scored(j)) + c, 0), c))
                     for c in range(LOCAL_CHUNKS)]
        in_specs += [pl.BlockSpec((Hk, V_AUG, blk),
                                  functools.partial(lambda c, j: (0, 0, base(finished(j)) + c), c))
                     for c in range(LOCAL_CHUNKS)]
        in_specs.append(pl.BlockSpec((1, Hb, LOCAL_CHUNKS * blk, blk),
                                     lambda j: (case(scored(j)), 0, 0, 0)))
        out_specs.append(pl.BlockSpec((Hq, HEAD_V, blk), lambda j: (0, 0, finished(j))))
        out_shapes.append(jax.ShapeDtypeStruct((Hq, HEAD_V, S), jnp.float32))
        groups.append((has_sink, q_per_kv))
        scratch += [pltpu.VMEM((Hq, LOCAL_CHUNKS * blk, blk), jnp.float32)] * 2
        scratch += [pltpu.VMEM((Hq, 1, blk), jnp.float32)] * 2
    return pl.pallas_call(
        functools.partial(_local_kernel, groups=tuple(groups)),
        grid=(nb + 1,),
        in_specs=in_specs,
        out_specs=out_specs,
        out_shape=out_shapes,
        scratch_shapes=scratch,
        compiler_params=_params(1),
        name="local",
    )(*operands)


def _swa_bias(nb):
    blk = LOCAL_BLK
    kk = np.arange(LOCAL_CHUNKS * blk)[:, None]
    qq = np.arange(blk)[None, :]
    out = []
    for q_off in (0, blk, 2 * blk):
        valid = np.abs(kk - (q_off + qq)) <= WINDOW
        out.append(np.where(valid, 0.0, NEG_INF))
    return jnp.asarray(np.stack(out)[:, None], jnp.float32)


def _na_bias(rpb, S):
    blk = LOCAL_BLK
    rows = S // GRID_W
    rpq = blk // GRID_W
    kpq = LOCAL_CHUNKS * rpq
    nb = S // blk
    n_off = 2 * NA_ROWS - 1
    kc = np.arange(GRID_W)[:, None]
    w = np.arange(GRID_W)[None, :]
    cs = np.clip(w - NA_COLS // 2, 0, GRID_W - NA_COLS)
    col_ok = (kc >= cs) & (kc < cs + NA_COLS)
    onehot = (kc - w + NA_COLS - 1)[None] == np.arange(2 * NA_COLS - 1)[:, None, None]
    onehot = jnp.asarray(onehot.reshape(2 * NA_COLS - 1, GRID_W * GRID_W), jnp.float32)
    block_of = np.full((3, kpq, rpq), n_off, np.int32)
    for c, i in enumerate((0, 1, nb - 1)):
        base_row = rpq * int(np.clip(i - 1, 0, nb - LOCAL_CHUNKS))
        for ki in range(kpq):
            for qj in range(rpq):
                r, kr = rpq * i + qj, base_row + ki
                rs = int(np.clip(r - NA_ROWS // 2, 0, rows - NA_ROWS))
                if rs <= kr < rs + NA_ROWS:
                    block_of[c, ki, qj] = kr - r + NA_ROWS - 1
    L, H = rpb.shape[:2]
    toe = jnp.einsum("lhab,bn->lhan", rpb, onehot, precision=jax.lax.Precision.HIGHEST)
    toe = toe.reshape(L, H, n_off, GRID_W, GRID_W) * LOG2E
    toe = jnp.where(jnp.asarray(col_ok), toe, NEG_INF)
    toe = jnp.concatenate([toe, jnp.full((L, H, 1, GRID_W, GRID_W), NEG_INF, jnp.float32)], axis=2)
    t = jnp.take(toe, jnp.asarray(block_of.reshape(-1)), axis=2)
    t = t.reshape(L, H, 3, kpq, rpq, GRID_W, GRID_W)
    t = jnp.transpose(t, (0, 2, 1, 3, 5, 4, 6))
    return t.reshape(L, 3, H, kpq * GRID_W, rpq * GRID_W)


def _post_kernel(x_ref, oa_ref, ob_ref, oc_ref, od_ref, ga_ref, gc_ref, gd_ref, gsub_ref,
                 lq1_ref, lk1_ref, lq2_ref, lk2_ref, linit_ref, w_out_ref, g2_ref, w_up_ref,
                 w_down_ref, gf_ref, y_ref, *, final):
    bf16 = jnp.bfloat16
    f32 = jnp.float32
    tokens = x_ref.shape[1]

    def heads(ref):
        return ref[...].reshape(ref.shape[0] * ref.shape[1], tokens)

    ya = _rms_rows(heads(oa_ref), ga_ref[...])
    yc = _rms_rows(heads(oc_ref), gc_ref[...])
    yd = _rms_rows(heads(od_ref), gd_ref[...])
    lam_init = linit_ref[...]
    lam = (jnp.exp(jnp.sum(lq1_ref[...] * lk1_ref[...], axis=1, keepdims=True))
           - jnp.exp(jnp.sum(lq2_ref[...] * lk2_ref[...], axis=1, keepdims=True)) + lam_init)
    yb = []
    for hd in range(DIFF_HEADS):
        w = ob_ref[2 * hd] - lam * ob_ref[2 * hd + 1]
        yb.append(_rms_rows(w, gsub_ref[...]) * (1.0 - lam_init))
    mix = jnp.concatenate([ya] + yb + [yc, yd], axis=0).astype(bf16)
    x1 = x_ref[...] + jnp.dot(w_out_ref[...], mix, preferred_element_type=f32)
    h2 = _rms_rows(x1, g2_ref[...]).astype(bf16)
    u = jnp.dot(w_up_ref[...], h2, preferred_element_type=f32)
    a = jnp.square(jnp.maximum(u, 0.0)).astype(bf16)
    x2 = x1 + jnp.dot(w_down_ref[...], a, preferred_element_type=f32)
    if final:
        x2 = _rms_rows(x2, gf_ref[...])
    y_ref[...] = x2


def _post(xT, oa, ob, oc, od, ga, gc, gd, gsub, lq1, lk1, lq2, lk2, linit, w_outT, g2, w_upT,
          w_downT, gf, final):
    S = xT.shape[1]
    tt = min(TOKEN_TILE, S)

    def tok(rows):
        return pl.BlockSpec((rows, tt), lambda i: (0, i))

    def heads(arr):
        return pl.BlockSpec((arr.shape[0], arr.shape[1], tt), lambda i: (0, 0, i))

    def resident(arr):
        nd = arr.ndim
        return pl.BlockSpec(arr.shape, lambda *_: (0,) * nd, pipeline_mode=pl.Buffered(1))

    small = [ga, gc, gd, gsub, lq1, lk1, lq2, lk2, linit]
    return pl.pallas_call(
        functools.partial(_post_kernel, final=final),
        grid=(S // tt,),
        in_specs=[tok(D_MODEL), heads(oa), heads(ob), heads(oc), heads(od)]
                 + [_const_spec(a.shape) for a in small]
                 + [resident(w_outT), _const_spec(g2.shape), resident(w_upT), resident(w_downT),
                    _const_spec(gf.shape)],
        out_specs=tok(D_MODEL),
        out_shape=jax.ShapeDtypeStruct((D_MODEL, S), jnp.float32),
        compiler_params=_params(1),
        name="post_final" if final else "post",
    )(xT, oa, ob, oc, od, *small, w_outT, g2, w_upT, w_downT, gf)


def _rope_tables_T(S, dim):
    inv = 1.0 / (ROPE_THETA ** (jnp.arange(0, dim, 2, dtype=jnp.float32) / dim))
    ang = inv[:, None] * jnp.arange(S, dtype=jnp.float32)[None, :]
    return jnp.cos(ang), jnp.sin(ang)


def _col(v):
    return v.astype(jnp.float32)[:, None]


def _wT(w):
    return jnp.swapaxes(w, -1, -2).astype(jnp.bfloat16)


@jax.jit
def _forward(x, norm1_g, w_in, mla_q_norm_g, mla_w_uq, mla_kv_norm_g, mla_w_uk, mla_w_uv,
             diff_lambda_q1, diff_lambda_k1, diff_lambda_q2, diff_lambda_k2, diff_subln_g,
             swa_sinks, na_rpb, out_g_mla, out_g_swa, out_g_na, w_out, norm2_g, w_up, w_down,
             final_norm_g):
    B, S, D = x.shape
    assert B == 1 and D == D_MODEL and S % TOKEN_TILE == 0 and S % GRID_W == 0
    c32, s32 = _rope_tables_T(S, MLA_ROPE_DIM)
    c64, s64 = _rope_tables_T(S, SWA_HEAD_DIM)
    swa_bias = _swa_bias(S // LOCAL_BLK)
    na_bias = _na_bias(na_rpb.astype(jnp.float32), S)
    xT = x[0].T
    gf = _col(final_norm_g)
    for l in range(DEPTH):
        prep = _prep(xT, _col(norm1_g[l]), _wT(w_in[l]), _col(mla_q_norm_g[l]), _wT(mla_w_uq[l]),
                     _col(mla_kv_norm_g[l]), _wT(mla_w_uk[l]), _wT(mla_w_uv[l]), c32, s32, c64, s64)
        mq, mk, mv, dq, dk, dv, sq, sk, sv, nq, nk, nv = prep
        oa = _dense_attention(mq, mk, mv, 1, "dense_mla")
        ob = _dense_attention(dq, dk, dv, 2, "dense_diff")
        oc, od = _local_attention(swa_sinks[l].astype(jnp.float32) * LOG2E,
                                  [(sq, sk, sv, swa_bias, SWA_GROUP, True),
                                   (nq, nk, nv, na_bias[l], 1, False)])
        lam_init = 0.8 - 0.6 * math.exp(-0.3 * l)
        row = lambda v: v.astype(jnp.float32)[None, :]
        xT = _post(xT, oa, ob, oc, od, _col(out_g_mla[l]), _col(out_g_swa[l]), _col(out_g_na[l]),
                   _col(diff_subln_g[l]), row(diff_lambda_q1[l]), row(diff_lambda_k1[l]),
                   row(diff_lambda_q2[l]), row(diff_lambda_k2[l]),
                   jnp.full((1, 1), lam_init, jnp.float32), _wT(w_out[l]), _col(norm2_g[l]),
                   _wT(w_up[l]), _wT(w_down[l]), gf, final=(l == DEPTH - 1))
    return xT.T[None]


def kernel(x, norm1_g, w_in, mla_q_norm_g, mla_w_uq, mla_kv_norm_g, mla_w_uk, mla_w_uv,
           diff_lambda_q1, diff_lambda_k1, diff_lambda_q2, diff_lambda_k2, diff_subln_g,
           swa_sinks, na_rpb, out_g_mla, out_g_swa, out_g_na, w_out, norm2_g, w_up, w_down,
           final_norm_g):
    return _forward(x, norm1_g, w_in, mla_q_norm_g, mla_w_uq, mla_kv_norm_g, mla_w_uk, mla_w_uv,
                    diff_lambda_q1, diff_lambda_k1, diff_lambda_q2, diff_lambda_k2, diff_subln_g,
                    swa_sinks, na_rpb, out_g_mla, out_g_swa, out_g_na, w_out, norm2_g, w_up,
                    w_down, final_norm_g)
```

```python
import functools
import math

import jax
import jax.numpy as jnp
import numpy as np
from jax.experimental import pallas as pl
from jax.experimental.pallas import tpu as pltpu

D_MODEL = 1024
DEPTH = 4
GRID_W = 64
ROPE_THETA = 10000.0
EPS = 1e-6
NEG_INF = -1e30

MLA_HEADS = 4
MLA_Q_RANK = 192
MLA_KV_RANK = 128
MLA_NOPE_DIM = 64
MLA_ROPE_DIM = 32
MLA_QK_DIM = MLA_NOPE_DIM + MLA_ROPE_DIM
DIFF_HEADS = 4
DIFF_QK_DIM = 32
SWA_Q_HEADS = 4
SWA_KV_HEADS = 2
SWA_GROUP = SWA_Q_HEADS // SWA_KV_HEADS
SWA_HEAD_DIM = 64
WINDOW = 128
NA_HEADS = 4
NA_HEAD_DIM = 64
NA_ROWS = 8
NA_COLS = 16
HEAD_V = 64

IN_SPLIT_WIDTHS = (
    MLA_Q_RANK, MLA_KV_RANK, MLA_ROPE_DIM,
    DIFF_HEADS * 2 * DIFF_QK_DIM, DIFF_HEADS * 2 * DIFF_QK_DIM, DIFF_HEADS * HEAD_V,
    SWA_Q_HEADS * SWA_HEAD_DIM, SWA_KV_HEADS * SWA_HEAD_DIM, SWA_KV_HEADS * SWA_HEAD_DIM,
    NA_HEADS * NA_HEAD_DIM, NA_HEADS * NA_HEAD_DIM, NA_HEADS * NA_HEAD_DIM,
)
IN_COLS = sum(IN_SPLIT_WIDTHS)
IN_OFFSETS = tuple(int(v) for v in np.cumsum((0,) + IN_SPLIT_WIDTHS))

V7X_BF16_SUBLANES = 16
V7X_VMEM_LIMIT_BYTES = 56 * 1024 * 1024

LOG2E = math.log2(math.e)
V_AUG = HEAD_V + V7X_BF16_SUBLANES

TOKEN_TILE = 512
DENSE_TQ = 512
DENSE_TILES_PER_STEP = 4
DENSE_TK = 8192
DENSE_SUB = 256
LOCAL_BLK = 256
LOCAL_CHUNKS = 3


def _params(n_axes):
    return pltpu.CompilerParams(
        dimension_semantics=("arbitrary",) * n_axes,
        vmem_limit_bytes=V7X_VMEM_LIMIT_BYTES,
    )


def _const_spec(shape):
    nd = len(shape)
    return pl.BlockSpec(shape, lambda *_: (0,) * nd)


def _rms_rows(x, g):
    r = jax.lax.rsqrt(jnp.mean(x * x, axis=0, keepdims=True) + EPS)
    return (x * r) * g


def _rope_rows(x1, x2, c, s):
    return x1 * c - x2 * s, x2 * c + x1 * s


def _ones_rows(tokens):
    row = jax.lax.broadcasted_iota(jnp.int32, (V7X_BF16_SUBLANES, tokens), 0)
    return jnp.where(row == 0, 1.0, 0.0).astype(jnp.bfloat16)


def _prep_kernel(x_ref, g1_ref, w_in_ref, qg_ref, w_uq_ref, kvg_ref, w_uk_ref, w_uv_ref,
                 c32_ref, s32_ref, c64_ref, s64_ref,
                 mq_ref, mk_ref, mv_ref, dq_ref, dk_ref, dv_ref,
                 sq_ref, sk_ref, sv_ref, nq_ref, nk_ref, nv_ref):
    bf16 = jnp.bfloat16
    f32 = jnp.float32
    tokens = x_ref.shape[1]
    h = _rms_rows(x_ref[...], g1_ref[...]).astype(bf16)
    proj = jnp.dot(w_in_ref[...], h, preferred_element_type=f32)
    part = [proj[IN_OFFSETS[i]:IN_OFFSETS[i + 1]] for i in range(len(IN_SPLIT_WIDTHS))]
    a_cq, a_ckv, a_kr, b_q, b_k, b_v, c_q, c_k, c_v, d_q, d_k, d_v = part
    c32, s32 = c32_ref[...], s32_ref[...]
    c64, s64 = c64_ref[...], s64_ref[...]
    ones = _ones_rows(tokens)

    mla_scale = MLA_QK_DIM ** -0.5 * LOG2E
    cqn = _rms_rows(a_cq, qg_ref[...]).astype(bf16)
    q_all = jnp.dot(w_uq_ref[...], cqn, preferred_element_type=f32)
    lat = _rms_rows(a_ckv, kvg_ref[...]).astype(bf16)
    k_nope = jnp.dot(w_uk_ref[...], lat, preferred_element_type=f32)
    v_all = jnp.dot(w_uv_ref[...], lat, preferred_element_type=f32)
    half = MLA_ROPE_DIM // 2
    kp1, kp2 = _rope_rows(a_kr[:half], a_kr[half:], c32, s32)
    for hd in range(MLA_HEADS):
        q = q_all[hd * MLA_QK_DIM:(hd + 1) * MLA_QK_DIM]
        q1, q2 = _rope_rows(q[MLA_NOPE_DIM:MLA_NOPE_DIM + half], q[MLA_NOPE_DIM + half:], c32, s32)
        mq_ref[hd, 0:MLA_NOPE_DIM] = (q[:MLA_NOPE_DIM] * mla_scale).astype(bf16)
        mq_ref[hd, MLA_NOPE_DIM:MLA_NOPE_DIM + half] = (q1 * mla_scale).astype(bf16)
        mq_ref[hd, MLA_NOPE_DIM + half:MLA_QK_DIM] = (q2 * mla_scale).astype(bf16)
        mk_ref[hd, 0:MLA_NOPE_DIM] = k_nope[hd * MLA_NOPE_DIM:(hd + 1) * MLA_NOPE_DIM].astype(bf16)
        mk_ref[hd, MLA_NOPE_DIM:MLA_NOPE_DIM + half] = kp1.astype(bf16)
        mk_ref[hd, MLA_NOPE_DIM + half:MLA_QK_DIM] = kp2.astype(bf16)
        mv_ref[hd, 0:HEAD_V] = v_all[hd * HEAD_V:(hd + 1) * HEAD_V].astype(bf16)
        mv_ref[hd, HEAD_V:V_AUG] = ones

    diff_scale = DIFF_QK_DIM ** -0.5 * LOG2E
    half = DIFF_QK_DIM // 2
    for m in range(2 * DIFF_HEADS):
        lo = m * DIFF_QK_DIM
        q1, q2 = _rope_rows(b_q[lo:lo + half], b_q[lo + half:lo + DIFF_QK_DIM], c32, s32)
        k1, k2 = _rope_rows(b_k[lo:lo + half], b_k[lo + half:lo + DIFF_QK_DIM], c32, s32)
        dq_ref[m, 0:half] = (q1 * diff_scale).astype(bf16)
        dq_ref[m, half:DIFF_QK_DIM] = (q2 * diff_scale).astype(bf16)
        dk_ref[m, 0:half] = k1.astype(bf16)
        dk_ref[m, half:DIFF_QK_DIM] = k2.astype(bf16)
    for hd in range(DIFF_HEADS):
        dv_ref[hd, 0:HEAD_V] = b_v[hd * HEAD_V:(hd + 1) * HEAD_V].astype(bf16)
        dv_ref[hd, HEAD_V:V_AUG] = ones

    swa_scale = SWA_HEAD_DIM ** -0.5 * LOG2E
    half = SWA_HEAD_DIM // 2
    for hd in range(SWA_Q_HEADS):
        lo = hd * SWA_HEAD_DIM
        q1, q2 = _rope_rows(c_q[lo:lo + half], c_q[lo + half:lo + SWA_HEAD_DIM], c64, s64)
        sq_ref[hd, 0:half] = (q1 * swa_scale).astype(bf16)
        sq_ref[hd, half:SWA_HEAD_DIM] = (q2 * swa_scale).astype(bf16)
    for hd in range(SWA_KV_HEADS):
        lo = hd * SWA_HEAD_DIM
        k1, k2 = _rope_rows(c_k[lo:lo + half], c_k[lo + half:lo + SWA_HEAD_DIM], c64, s64)
        sk_ref[hd, 0:half] = k1.astype(bf16)
        sk_ref[hd, half:SWA_HEAD_DIM] = k2.astype(bf16)
        sv_ref[hd, 0:HEAD_V] = c_v[lo:lo + HEAD_V].astype(bf16)
        sv_ref[hd, HEAD_V:V_AUG] = ones

    na_scale = NA_HEAD_DIM ** -0.5 * LOG2E
    for hd in range(NA_HEADS):
        lo = hd * NA_HEAD_DIM
        nq_ref[hd] = (d_q[lo:lo + NA_HEAD_DIM] * na_scale).astype(bf16)
        nk_ref[hd] = d_k[lo:lo + NA_HEAD_DIM].astype(bf16)
        nv_ref[hd, 0:HEAD_V] = d_v[lo:lo + HEAD_V].astype(bf16)
        nv_ref[hd, HEAD_V:V_AUG] = ones


def _prep(xT, g1, w_inT, qg, w_uqT, kvg, w_ukT, w_uvT, c32, s32, c64, s64):
    S = xT.shape[1]
    tt = min(TOKEN_TILE, S)
    bf16 = jnp.bfloat16

    def tok(rows):
        return pl.BlockSpec((rows, tt), lambda i: (0, i))

    def head_out(heads, rows):
        return (jax.ShapeDtypeStruct((heads, rows, S), bf16),
                pl.BlockSpec((heads, rows, tt), lambda i: (0, 0, i)))

    outs = [head_out(MLA_HEADS, MLA_QK_DIM), head_out(MLA_HEADS, MLA_QK_DIM), head_out(MLA_HEADS, V_AUG),
            head_out(2 * DIFF_HEADS, DIFF_QK_DIM), head_out(2 * DIFF_HEADS, DIFF_QK_DIM),
            head_out(DIFF_HEADS, V_AUG),
            head_out(SWA_Q_HEADS, SWA_HEAD_DIM), head_out(SWA_KV_HEADS, SWA_HEAD_DIM),
            head_out(SWA_KV_HEADS, V_AUG),
            head_out(NA_HEADS, NA_HEAD_DIM), head_out(NA_HEADS, NA_HEAD_DIM), head_out(NA_HEADS, V_AUG)]
    return pl.pallas_call(
        _prep_kernel,
        grid=(S // tt,),
        in_specs=[tok(D_MODEL), _const_spec(g1.shape), _const_spec(w_inT.shape), _const_spec(qg.shape),
                  _const_spec(w_uqT.shape), _const_spec(kvg.shape), _const_spec(w_ukT.shape),
                  _const_spec(w_uvT.shape), tok(c32.shape[0]), tok(s32.shape[0]), tok(c64.shape[0]),
                  tok(s64.shape[0])],
        out_specs=[o[1] for o in outs],
        out_shape=[o[0] for o in outs],
        compiler_params=_params(1),
        name="prep",
    )(xT, g1, w_inT, qg, w_uqT, kvg, w_ukT, w_uvT, c32, s32, c64, s64)


def _dense_kernel(q_ref, qn_ref, k_ref, v_ref, o_ref, acc_ref, m_ref,
                  s0_ref, s1_ref, b0_ref, b1_ref, qq_ref):
    f32 = jnp.float32
    nk, tk = k_ref.shape[1], k_ref.shape[2]
    tq = qn_ref.shape[2]
    n_tiles = q_ref.shape[2] // tq
    first_tile = pl.program_id(1) == 0
    for g in range(n_tiles):
        qq_ref[g] = q_ref[0, :, g * tq:(g + 1) * tq]
    qq_ref[n_tiles] = qn_ref[0]
    sub = min(DENSE_SUB, tk)
    subs = [slice(c * sub, (c + 1) * sub) for c in range(tk // sub)]
    bufs = ((s0_ref, b0_ref), (s1_ref, b1_ref))

    def region(q_a, kb_a, kb_c, parity, m_old):
        s_a, b_a = bufs[1 - parity]
        s_c, b_c = bufs[parity]
        if kb_c is not None:
            m_new = jnp.maximum(m_old, b_c[...])
            alpha = jnp.exp2(m_old - m_new)
            m_ref[...] = m_new
        bmax = pv = None
        for c in subs:
            s = jnp.dot(k_ref[0, kb_a, c, :], q_a, preferred_element_type=f32)
            s_a[c, :] = s
            mc = jnp.max(s, axis=0, keepdims=True)
            bmax = mc if bmax is None else jnp.maximum(bmax, mc)
            if kb_c is not None:
                p = jnp.exp2(s_c[c, :] - m_new).astype(jnp.bfloat16)
                v = v_ref[0, :, pl.ds(pl.multiple_of(kb_c * tk + c.start, sub), sub)]
                t = jnp.dot(v, p, preferred_element_type=f32)
                pv = t if pv is None else pv + t
        b_a[...] = bmax
        if kb_c is not None:
            acc_ref[...] = alpha * acc_ref[...] + pv

    @pl.when(first_tile)
    def _():
        acc_ref[...] = jnp.zeros_like(acc_ref)
        m_ref[...] = jnp.full_like(m_ref, NEG_INF)
        region(qq_ref[0], 0, None, 1, None)

    def block(t, carry):
        tile = jax.lax.div(t, nk)
        kb = t - tile * nk
        last = kb == nk - 1
        for parity in (0, 1):
            @pl.when(t % 2 == parity)
            def _():
                q_a = qq_ref[tile + jnp.where(last, 1, 0)]
                kb_a = jnp.where(last, 0, kb + 1)
                m_old = jnp.where(kb == 0, NEG_INF, m_ref[...])
                region(q_a, kb_a, kb, parity, m_old)

        @pl.when(last)
        def _():
            acc = acc_ref[...]
            o_ref[0, :, pl.ds(pl.multiple_of(tile * tq, tq), tq)] = (
                acc[0:HEAD_V] / acc[HEAD_V:HEAD_V + 1])
            acc_ref[...] = jnp.zeros_like(acc_ref)
        return carry

    jax.lax.fori_loop(0, n_tiles * nk, block, 0)


def _dense_attention(qT, kT, vT, maps_per_value, name):
    M, d, S = qT.shape
    tq = min(DENSE_TQ, S)
    tk = min(DENSE_TK, S)
    nk, nq = S // tk, S // tq
    tiles = min(DENSE_TILES_PER_STEP, nq)
    assert nk >= 2 and nk % 2 == 0 and nq % tiles == 0
    k = jnp.swapaxes(kT, 1, 2).reshape(M, nk, tk, d)
    row = pltpu.VMEM((1, tq), jnp.float32)
    return pl.pallas_call(
        _dense_kernel,
        grid=(M, nq // tiles),
        in_specs=[pl.BlockSpec((1, d, tiles * tq), lambda m, i: (m, 0, i)),
                  pl.BlockSpec((1, d, tq), lambda m, i: (m, 0, jnp.minimum(tiles * (i + 1), nq - 1))),
                  pl.BlockSpec((1, nk, tk, d), lambda m, i: (m, 0, 0, 0)),
                  pl.BlockSpec((1, V_AUG, S), lambda m, i: (m // maps_per_value, 0, 0))],
        out_specs=pl.BlockSpec((1, HEAD_V, tiles * tq), lambda m, i: (m, 0, i)),
        out_shape=jax.ShapeDtypeStruct((M, HEAD_V, S), jnp.float32),
        scratch_shapes=[pltpu.VMEM((V_AUG, tq), jnp.float32), row,
                        pltpu.VMEM((tk, tq), jnp.float32), pltpu.VMEM((tk, tq), jnp.float32),
                        row, row, pltpu.VMEM((tiles + 1, d, tq), qT.dtype)],
        compiler_params=_params(2),
        name=name,
    )(qT, qT, k, vT)


def _local_kernel(sink_ref, *refs, groups):
    f32 = jnp.float32
    n_in, n_g = 8, len(groups)
    outs = refs[n_in * n_g:(n_in + 1) * n_g]
    scratch = refs[(n_in + 1) * n_g:]
    j = pl.program_id(0)
    n_heads = max(refs[n_in * g].shape[0] for g in range(n_g))

    @pl.when(j == 0)
    def _():
        for g in range(n_g):
            scratch[4 * g + 1][...] = jnp.zeros_like(scratch[4 * g + 1])
            scratch[4 * g + 3][...] = jnp.zeros_like(scratch[4 * g + 3])

    def step(parity):
        for h in range(n_heads):
            for g, (has_sink, q_per_kv) in enumerate(groups):
                q_ref, k0_ref, k1_ref, k2_ref, v0_ref, v1_ref, v2_ref, b_ref = refs[n_in * g:n_in * (g + 1)]
                if h >= q_ref.shape[0]:
                    continue
                s_a, s_b = scratch[4 * g + parity], scratch[4 * g + 1 - parity]
                m_a, m_b = scratch[4 * g + 2 + parity], scratch[4 * g + 3 - parity]
                blk = q_ref.shape[2]
                chunks = [slice(c * blk, (c + 1) * blk) for c in range(LOCAL_CHUNKS)]
                kv = h // q_per_kv
                q = q_ref[h]
                m = None
                for c, k in zip(chunks, (k0_ref, k1_ref, k2_ref)):
                    s = jnp.dot(k[kv], q, preferred_element_type=f32) + b_ref[0, h % b_ref.shape[1], c]
                    s_a[h, c] = s
                    mc = jnp.max(s, axis=0, keepdims=True)
                    m = mc if m is None else jnp.maximum(m, mc)
                if has_sink:
                    m = jnp.maximum(m, sink_ref[h])
                m_a[h] = m
                m = m_b[h]
                pv = None
                for c, v in zip(chunks, (v0_ref, v1_ref, v2_ref)):
                    p = jnp.exp2(s_b[h, c] - m).astype(jnp.bfloat16)
                    t = jnp.dot(v[kv], p, preferred_element_type=f32)
                    pv = t if pv is None else pv + t
                l = pv[HEAD_V:HEAD_V + 1]
                if has_sink:
                    l = l + jnp.exp2(sink_ref[h] - m)
                outs[g][h] = pv[0:HEAD_V] / l

    for parity in (0, 1):
        @pl.when(j % 2 == parity)
        def _():
            step(parity)


def _local_attention(sinks, group_args):
    S = group_args[0][0].shape[2]
    blk = LOCAL_BLK
    nb = S // blk
    assert nb >= LOCAL_CHUNKS

    def base(i):
        return jnp.clip(i - 1, 0, nb - LOCAL_CHUNKS)

    def case(i):
        return jnp.where(i == 0, 0, jnp.where(i == nb - 1, 2, 1))

    def scored(j):
        return jnp.minimum(j, nb - 1)

    def finished(j):
        return jnp.maximum(j - 1, 0)

    operands, in_specs, out_specs, out_shapes, groups, scratch = (
        [sinks], [pl.BlockSpec(memory_space=pltpu.SMEM)], [], [], [], [])
    for qT, kT, vT, bias, q_per_kv, has_sink in group_args:
        Hq, d, _ = qT.shape
        Hk, Hb = kT.shape[0], bias.shape[1]
        k = jnp.swapaxes(kT, 1, 2)
        operands += [qT, k, k, k, vT, vT, vT, bias]
        in_specs.append(pl.BlockSpec((Hq, d, blk), lambda j: (0, 0, scored(j))))
        in_specs += [pl.BlockSpec((Hk, blk, d),
                                  functools.partial(lambda c, j: (0, base(scored(j)) + c, 0), c))
                     for c in range(LOCAL_CHUNKS)]
        in_specs += [pl.BlockSpec((Hk, V_AUG, blk),
                                  functools.partial(lambda c, j: (0, 0, base(finished(j)) + c), c))
                     for c in range(LOCAL_CHUNKS)]
        in_specs.append(pl.BlockSpec((1, Hb, LOCAL_CHUNKS * blk, blk),
                                     lambda j: (case(scored(j)), 0, 0, 0)))
        out_specs.append(pl.BlockSpec((Hq, HEAD_V, blk), lambda j: (0, 0, finished(j))))
        out_shapes.append(jax.ShapeDtypeStruct((Hq, HEAD_V, S), jnp.float32))
        groups.append((has_sink, q_per_kv))
        scratch += [pltpu.VMEM((Hq, LOCAL_CHUNKS * blk, blk), jnp.float32)] * 2
        scratch += [pltpu.VMEM((Hq, 1, blk), jnp.float32)] * 2
    return pl.pallas_call(
        functools.partial(_local_kernel, groups=tuple(groups)),
        grid=(nb + 1,),
        in_specs=in_specs,
        out_specs=out_specs,
        out_shape=out_shapes,
        scratch_shapes=scratch,
        compiler_params=_params(1),
        name="local",
    )(*operands)


def _swa_bias(nb):
    blk = LOCAL_BLK
    kk = np.arange(LOCAL_CHUNKS * blk)[:, None]
    qq = np.arange(blk)[None, :]
    out = []
    for q_off in (0, blk, 2 * blk):
        valid = np.abs(kk - (q_off + qq)) <= WINDOW
        out.append(np.where(valid, 0.0, NEG_INF))
    return jnp.asarray(np.stack(out)[:, None], jnp.float32)


def _na_bias(rpb, S):
    blk = LOCAL_BLK
    rows = S // GRID_W
    rpq = blk // GRID_W
    kpq = LOCAL_CHUNKS * rpq
    nb = S // blk
    n_off = 2 * NA_ROWS - 1
    kc = np.arange(GRID_W)[:, None]
    w = np.arange(GRID_W)[None, :]
    cs = np.clip(w - NA_COLS // 2, 0, GRID_W - NA_COLS)
    col_ok = (kc >= cs) & (kc < cs + NA_COLS)
    onehot = (kc - w + NA_COLS - 1)[None] == np.arange(2 * NA_COLS - 1)[:, None, None]
    onehot = jnp.asarray(onehot.reshape(2 * NA_COLS - 1, GRID_W * GRID_W), jnp.float32)
    block_of = np.full((3, kpq, rpq), n_off, np.int32)
    for c, i in enumerate((0, 1, nb - 1)):
        base_row = rpq * int(np.clip(i - 1, 0, nb - LOCAL_CHUNKS))
        for ki in range(kpq):
            for qj in range(rpq):
                r, kr = rpq * i + qj, base_row + ki
                rs = int(np.clip(r - NA_ROWS // 2, 0, rows - NA_ROWS))
                if rs <= kr < rs + NA_ROWS:
                    block_of[c, ki, qj] = kr - r + NA_ROWS - 1
    L, H = rpb.shape[:2]
    toe = jnp.einsum("lhab,bn->lhan", rpb, onehot, precision=jax.lax.Precision.HIGHEST)
    toe = toe.reshape(L, H, n_off, GRID_W, GRID_W) * LOG2E
    toe = jnp.where(jnp.asarray(col_ok), toe, NEG_INF)
    toe = jnp.concatenate([toe, jnp.full((L, H, 1, GRID_W, GRID_W), NEG_INF, jnp.float32)], axis=2)
    t = jnp.take(toe, jnp.asarray(block_of.reshape(-1)), axis=2)
    t = t.reshape(L, H, 3, kpq, rpq, GRID_W, GRID_W)
    t = jnp.transpose(t, (0, 2, 1, 3, 5, 4, 6))
    return t.reshape(L, 3, H, kpq * GRID_W, rpq * GRID_W)


def _post_kernel(x_ref, oa_ref, ob_ref, oc_ref, od_ref, ga_ref, gc_ref, gd_ref, gsub_ref,
                 lq1_ref, lk1_ref, lq2_ref, lk2_ref, linit_ref, w_out_ref, g2_ref, w_up_ref,
                 w_down_ref, gf_ref, y_ref, *, final):
    bf16 = jnp.bfloat16
    f32 = jnp.float32
    tokens = x_ref.shape[1]

    def heads(ref):
        return ref[...].reshape(ref.shape[0] * ref.shape[1], tokens)

    ya = _rms_rows(heads(oa_ref), ga_ref[...])
    yc = _rms_rows(heads(oc_ref), gc_ref[...])
    yd = _rms_rows(heads(od_ref), gd_ref[...])
    lam_init = linit_ref[...]
    lam = (jnp.exp(jnp.sum(lq1_ref[...] * lk1_ref[...], axis=1, keepdims=True))
           - jnp.exp(jnp.sum(lq2_ref[...] * lk2_ref[...], axis=1, keepdims=True)) + lam_init)
    yb = []
    for hd in range(DIFF_HEADS):
        w = ob_ref[2 * hd] - lam * ob_ref[2 * hd + 1]
        yb.append(_rms_rows(w, gsub_ref[...]) * (1.0 - lam_init))
    mix = jnp.concatenate([ya] + yb + [yc, yd], axis=0).astype(bf16)
    x1 = x_ref[...] + jnp.dot(w_out_ref[...], mix, preferred_element_type=f32)
    h2 = _rms_rows(x1, g2_ref[...]).astype(bf16)
    u = jnp.dot(w_up_ref[...], h2, preferred_element_type=f32)
    a = jnp.square(jnp.maximum(u, 0.0)).astype(bf16)
    x2 = x1 + jnp.dot(w_down_ref[...], a, preferred_element_type=f32)
    if final:
        x2 = _rms_rows(x2, gf_ref[...])
    y_ref[...] = x2


def _post(xT, oa, ob, oc, od, ga, gc, gd, gsub, lq1, lk1, lq2, lk2, linit, w_outT, g2, w_upT,
          w_downT, gf, final):
    S = xT.shape[1]
    tt = min(TOKEN_TILE, S)

    def tok(rows):
        return pl.BlockSpec((rows, tt), lambda i: (0, i))

    def heads(arr):
        return pl.BlockSpec((arr.shape[0], arr.shape[1], tt), lambda i: (0, 0, i))

    def resident(arr):
        nd = arr.ndim
        return pl.BlockSpec(arr.shape, lambda *_: (0,) * nd, pipeline_mode=pl.Buffered(1))

    small = [ga, gc, gd, gsub, lq1, lk1, lq2, lk2, linit]
    return pl.pallas_call(
        functools.partial(_post_kernel, final=final),
        grid=(S // tt,),
        in_specs=[tok(D_MODEL), heads(oa), heads(ob), heads(oc), heads(od)]
                 + [_const_spec(a.shape) for a in small]
                 + [resident(w_outT), _const_spec(g2.shape), resident(w_upT), resident(w_downT),
                    _const_spec(gf.shape)],
        out_specs=tok(D_MODEL),
        out_shape=jax.ShapeDtypeStruct((D_MODEL, S), jnp.float32),
        compiler_params=_params(1),
        name="post_final" if final else "post",
    )(xT, oa, ob, oc, od, *small, w_outT, g2, w_upT, w_downT, gf)


def _rope_tables_T(S, dim):
    inv = 1.0 / (ROPE_THETA ** (jnp.arange(0, dim, 2, dtype=jnp.float32) / dim))
    ang = inv[:, None] * jnp.arange(S, dtype=jnp.float32)[None, :]
    return jnp.cos(ang), jnp.sin(ang)


def _col(v):
    return v.astype(jnp.float32)[:, None]


def _wT(w):
    return jnp.swapaxes(w, -1, -2).astype(jnp.bfloat16)


@jax.jit
def _forward(x, norm1_g, w_in, mla_q_norm_g, mla_w_uq, mla_kv_norm_g, mla_w_uk, mla_w_uv,
             diff_lambda_q1, diff_lambda_k1, diff_lambda_q2, diff_lambda_k2, diff_subln_g,
             swa_sinks, na_rpb, out_g_mla, out_g_swa, out_g_na, w_out, norm2_g, w_up, w_down,
             final_norm_g):
    B, S, D = x.shape
    assert B == 1 and D == D_MODEL and S % TOKEN_TILE == 0 and S % GRID_W == 0
    c32, s32 = _rope_tables_T(S, MLA_ROPE_DIM)
    c64, s64 = _rope_tables_T(S, SWA_HEAD_DIM)
    swa_bias = _swa_bias(S // LOCAL_BLK)
    na_bias = _na_bias(na_rpb.astype(jnp.float32), S)
    xT = x[0].T
    gf = _col(final_norm_g)
    for l in range(DEPTH):
        prep = _prep(xT, _col(norm1_g[l]), _wT(w_in[l]), _col(mla_q_norm_g[l]), _wT(mla_w_uq[l]),
                     _col(mla_kv_norm_g[l]), _wT(mla_w_uk[l]), _wT(mla_w_uv[l]), c32, s32, c64, s64)
        mq, mk, mv, dq, dk, dv, sq, sk, sv, nq, nk, nv = prep
        oa = _dense_attention(mq, mk, mv, 1, "dense_mla")
        ob = _dense_attention(dq, dk, dv, 2, "dense_diff")
        oc, od = _local_attention(swa_sinks[l].astype(jnp.float32) * LOG2E,
                                  [(sq, sk, sv, swa_bias, SWA_GROUP, True),
                                   (nq, nk, nv, na_bias[l], 1, False)])
        lam_init = 0.8 - 0.6 * math.exp(-0.3 * l)
        row = lambda v: v.astype(jnp.float32)[None, :]
        xT = _post(xT, oa, ob, oc, od, _col(out_g_mla[l]), _col(out_g_swa[l]), _col(out_g_na[l]),
                   _col(diff_subln_g[l]), row(diff_lambda_q1[l]), row(diff_lambda_k1[l]),
                   row(diff_lambda_q2[l]), row(diff_lambda_k2[l]),
                   jnp.full((1, 1), lam_init, jnp.float32), _wT(w_out[l]), _col(norm2_g[l]),
                   _wT(w_up[l]), _wT(w_down[l]), gf, final=(l == DEPTH - 1))
    return xT.T[None]


def kernel(x, norm1_g, w_in, mla_q_norm_g, mla_w_uq, mla_kv_norm_g, mla_w_uk, mla_w_uv,
           diff_lambda_q1, diff_lambda_k1, diff_lambda_q2, diff_lambda_k2, diff_subln_g,
           swa_sinks, na_rpb, out_g_mla, out_g_swa, out_g_na, w_out, norm2_g, w_up, w_down,
           final_norm_g):
    return _forward(x, norm1_g, w_in, mla_q_norm_g, mla_w_uq, mla_kv_norm_g, mla_w_uk, mla_w_uv,
                    diff_lambda_q1, diff_lambda_k1, diff_lambda_q2, diff_lambda_k2, diff_subln_g,
                    swa_sinks, na_rpb, out_g_mla, out_g_swa, out_g_na, w_out, norm2_g, w_up,
                    w_down, final_norm_g)
```

```python
import functools
import math

import jax
import jax.numpy as jnp
import numpy as np
from jax.experimental import pallas as pl
from jax.experimental.pallas import tpu as pltpu

D_MODEL = 1024
DEPTH = 4
GRID_W = 64
ROPE_THETA = 10000.0
EPS = 1e-6
NEG_INF = -1e30

MLA_HEADS = 4
MLA_Q_RANK = 192
MLA_KV_RANK = 128
MLA_NOPE_DIM = 64
MLA_ROPE_DIM = 32
MLA_QK_DIM = MLA_NOPE_DIM + MLA_ROPE_DIM
DIFF_HEADS = 4
DIFF_QK_DIM = 32
SWA_Q_HEADS = 4
SWA_KV_HEADS = 2
SWA_GROUP = SWA_Q_HEADS // SWA_KV_HEADS
SWA_HEAD_DIM = 64
WINDOW = 128
NA_HEADS = 4
NA_HEAD_DIM = 64
NA_ROWS = 8
NA_COLS = 16
HEAD_V = 64

IN_SPLIT_WIDTHS = (
    MLA_Q_RANK, MLA_KV_RANK, MLA_ROPE_DIM,
    DIFF_HEADS * 2 * DIFF_QK_DIM, DIFF_HEADS * 2 * DIFF_QK_DIM, DIFF_HEADS * HEAD_V,
    SWA_Q_HEADS * SWA_HEAD_DIM, SWA_KV_HEADS * SWA_HEAD_DIM, SWA_KV_HEADS * SWA_HEAD_DIM,
    NA_HEADS * NA_HEAD_DIM, NA_HEADS * NA_HEAD_DIM, NA_HEADS * NA_HEAD_DIM,
)
IN_COLS = sum(IN_SPLIT_WIDTHS)
IN_OFFSETS = tuple(int(v) for v in np.cumsum((0,) + IN_SPLIT_WIDTHS))

V7X_BF16_SUBLANES = 16
V7X_VMEM_LIMIT_BYTES = 56 * 1024 * 1024

LOG2E = math.log2(math.e)
V_AUG = HEAD_V + V7X_BF16_SUBLANES

TOKEN_TILE = 512
DENSE_TQ = 512
DENSE_TILES_PER_STEP = 4
DENSE_TK = 8192
DENSE_SUB = 256
LOCAL_BLK = 256
LOCAL_CHUNKS = 3


def _params(n_axes):
    return pltpu.CompilerParams(
        dimension_semantics=("arbitrary",) * n_axes,
        vmem_limit_bytes=V7X_VMEM_LIMIT_BYTES,
    )


def _const_spec(shape):
    nd = len(shape)
    return pl.BlockSpec(shape, lambda *_: (0,) * nd)


def _rms_rows(x, g):
    r = jax.lax.rsqrt(jnp.mean(x * x, axis=0, keepdims=True) + EPS)
    return (x * r) * g


def _rope_rows(x1, x2, c, s):
    return x1 * c - x2 * s, x2 * c + x1 * s


def _ones_rows(tokens):
    row = jax.lax.broadcasted_iota(jnp.int32, (V7X_BF16_SUBLANES, tokens), 0)
    return jnp.where(row == 0, 1.0, 0.0).astype(jnp.bfloat16)


def _prep_kernel(x_ref, g1_ref, w_in_ref, qg_ref, w_uq_ref, kvg_ref, w_uk_ref, w_uv_ref,
                 c32_ref, s32_ref, c64_ref, s64_ref,
                 mq_ref, mk_ref, mv_ref, dq_ref, dk_ref, dv_ref,
                 sq_ref, sk_ref, sv_ref, nq_ref, nk_ref, nv_ref):
    bf16 = jnp.bfloat16
    f32 = jnp.float32
    tokens = x_ref.shape[1]
    h = _rms_rows(x_ref[...], g1_ref[...]).astype(bf16)
    proj = jnp.dot(w_in_ref[...], h, preferred_element_type=f32)
    part = [proj[IN_OFFSETS[i]:IN_OFFSETS[i + 1]] for i in range(len(IN_SPLIT_WIDTHS))]
    a_cq, a_ckv, a_kr, b_q, b_k, b_v, c_q, c_k, c_v, d_q, d_k, d_v = part
    c32, s32 = c32_ref[...], s32_ref[...]
    c64, s64 = c64_ref[...], s64_ref[...]
    ones = _ones_rows(tokens)

    mla_scale = MLA_QK_DIM ** -0.5 * LOG2E
    cqn = _rms_rows(a_cq, qg_ref[...]).astype(bf16)
    q_all = jnp.dot(w_uq_ref[...], cqn, preferred_element_type=f32)
    lat = _rms_rows(a_ckv, kvg_ref[...]).astype(bf16)
    k_nope = jnp.dot(w_uk_ref[...], lat, preferred_element_type=f32)
    v_all = jnp.dot(w_uv_ref[...], lat, preferred_element_type=f32)
    half = MLA_ROPE_DIM // 2
    kp1, kp2 = _rope_rows(a_kr[:half], a_kr[half:], c32, s32)
    for hd in range(MLA_HEADS):
        q = q_all[hd * MLA_QK_DIM:(hd + 1) * MLA_QK_DIM]
        q1, q2 = _rope_rows(q[MLA_NOPE_DIM:MLA_NOPE_DIM + half], q[MLA_NOPE_DIM + half:], c32, s32)
        mq_ref[hd, 0:MLA_NOPE_DIM] = (q[:MLA_NOPE_DIM] * mla_scale).astype(bf16)
        mq_ref[hd, MLA_NOPE_DIM:MLA_NOPE_DIM + half] = (q1 * mla_scale).astype(bf16)
        mq_ref[hd, MLA_NOPE_DIM + half:MLA_QK_DIM] = (q2 * mla_scale).astype(bf16)
        mk_ref[hd, 0:MLA_NOPE_DIM] = k_nope[hd * MLA_NOPE_DIM:(hd + 1) * MLA_NOPE_DIM].astype(bf16)
        mk_ref[hd, MLA_NOPE_DIM:MLA_NOPE_DIM + half] = kp1.astype(bf16)
        mk_ref[hd, MLA_NOPE_DIM + half:MLA_QK_DIM] = kp2.astype(bf16)
        mv_ref[hd, 0:HEAD_V] = v_all[hd * HEAD_V:(hd + 1) * HEAD_V].astype(bf16)
        mv_ref[hd, HEAD_V:V_AUG] = ones

    diff_scale = DIFF_QK_DIM ** -0.5 * LOG2E
    half = DIFF_QK_DIM // 2
    for m in range(2 * DIFF_HEADS):
        lo = m * DIFF_QK_DIM
        q1, q2 = _rope_rows(b_q[lo:lo + half], b_q[lo + half:lo + DIFF_QK_DIM], c32, s32)
        k1, k2 = _rope_rows(b_k[lo:lo + half], b_k[lo + half:lo + DIFF_QK_DIM], c32, s32)
        dq_ref[m, 0:half] = (q1 * diff_scale).astype(bf16)
        dq_ref[m, half:DIFF_QK_DIM] = (q2 * diff_scale).astype(bf16)
        dk_ref[m, 0:half] = k1.astype(bf16)
        dk_ref[m, half:DIFF_QK_DIM] = k2.astype(bf16)
    for hd in range(DIFF_HEADS):
        dv_ref[hd, 0:HEAD_V] = b_v[hd * HEAD_V:(hd + 1) * HEAD_V].astype(bf16)
        dv_ref[hd, HEAD_V:V_AUG] = ones

    swa_scale = SWA_HEAD_DIM ** -0.5 * LOG2E
    half = SWA_HEAD_DIM // 2
    for hd in range(SWA_Q_HEADS):
        lo = hd * SWA_HEAD_DIM
        q1, q2 = _rope_rows(c_q[lo:lo + half], c_q[lo + half:lo + SWA_HEAD_DIM], c64, s64)
        sq_ref[hd, 0:half] = (q1 * swa_scale).astype(bf16)
        sq_ref[hd, half:SWA_HEAD_DIM] = (q2 * swa_scale).astype(bf16)
    for hd in range(SWA_KV_HEADS):
        lo = hd * SWA_HEAD_DIM
        k1, k2 = _rope_rows(c_k[lo:lo + half], c_k[lo + half:lo + SWA_HEAD_DIM], c64, s64)
        sk_ref[hd, 0:half] = k1.astype(bf16)
        sk_ref[hd, half:SWA_HEAD_DIM] = k2.astype(bf16)
        sv_ref[hd, 0:HEAD_V] = c_v[lo:lo + HEAD_V].astype(bf16)
        sv_ref[hd, HEAD_V:V_AUG] = ones

    na_scale = NA_HEAD_DIM ** -0.5 * LOG2E
    for hd in range(NA_HEADS):
        lo = hd * NA_HEAD_DIM
        nq_ref[hd] = (d_q[lo:lo + NA_HEAD_DIM] * na_scale).astype(bf16)
        nk_ref[hd] = d_k[lo:lo + NA_HEAD_DIM].astype(bf16)
        nv_ref[hd, 0:HEAD_V] = d_v[lo:lo + HEAD_V].astype(bf16)
        nv_ref[hd, HEAD_V:V_AUG] = ones


def _prep(xT, g1, w_inT, qg, w_uqT, kvg, w_ukT, w_uvT, c32, s32, c64, s64):
    S = xT.shape[1]
    tt = min(TOKEN_TILE, S)
    bf16 = jnp.bfloat16

    def tok(rows):
        return pl.BlockSpec((rows, tt), lambda i: (0, i))

    def head_out(heads, rows):
        return (jax.ShapeDtypeStruct((heads, rows, S), bf16),
                pl.BlockSpec((heads, rows, tt), lambda i: (0, 0, i)))

    outs = [head_out(MLA_HEADS, MLA_QK_DIM), head_out(MLA_HEADS, MLA_QK_DIM), head_out(MLA_HEADS, V_AUG),
            head_out(2 * DIFF_HEADS, DIFF_QK_DIM), head_out(2 * DIFF_HEADS, DIFF_QK_DIM),
            head_out(DIFF_HEADS, V_AUG),
            head_out(SWA_Q_HEADS, SWA_HEAD_DIM), head_out(SWA_KV_HEADS, SWA_HEAD_DIM),
            head_out(SWA_KV_HEADS, V_AUG),
            head_out(NA_HEADS, NA_HEAD_DIM), head_out(NA_HEADS, NA_HEAD_DIM), head_out(NA_HEADS, V_AUG)]
    return pl.pallas_call(
        _prep_kernel,
        grid=(S // tt,),
        in_specs=[tok(D_MODEL), _const_spec(g1.shape), _const_spec(w_inT.shape), _const_spec(qg.shape),
                  _const_spec(w_uqT.shape), _const_spec(kvg.shape), _const_spec(w_ukT.shape),
                  _const_spec(w_uvT.shape), tok(c32.shape[0]), tok(s32.shape[0]), tok(c64.shape[0]),
                  tok(s64.shape[0])],
        out_specs=[o[1] for o in outs],
        out_shape=[o[0] for o in outs],
        compiler_params=_params(1),
        name="prep",
    )(xT, g1, w_inT, qg, w_uqT, kvg, w_ukT, w_uvT, c32, s32, c64, s64)


def _dense_kernel(q_ref, qn_ref, k_ref, kn_ref, v_ref, o_ref, acc_ref, m_ref,
                  s0_ref, s1_ref, b0_ref, b1_ref, qq_ref):
    f32 = jnp.float32
    nk, tk = k_ref.shape[1], k_ref.shape[2]
    tq = qn_ref.shape[2]
    n_tiles = q_ref.shape[2] // tq
    stream_start = jnp.logical_and(pl.program_id(0) == 0, pl.program_id(1) == 0)
    for g in range(n_tiles):
        qq_ref[g] = q_ref[0, :, g * tq:(g + 1) * tq]
    qq_ref[n_tiles] = qn_ref[0]
    sub = min(DENSE_SUB, tk)
    subs = [slice(c * sub, (c + 1) * sub) for c in range(tk // sub)]
    bufs = ((s0_ref, b0_ref), (s1_ref, b1_ref))

    def region(q_a, k_a, kb_c, parity, m_old):
        s_a, b_a = bufs[1 - parity]
        s_c, b_c = bufs[parity]
        if kb_c is not None:
            m_new = jnp.maximum(m_old, b_c[...])
            alpha = jnp.exp2(m_old - m_new)
            m_ref[...] = m_new
        bmax = pv = None
        for c in subs:
            s = jnp.dot(k_a(c), q_a, preferred_element_type=f32)
            s_a[c, :] = s
            mc = jnp.max(s, axis=0, keepdims=True)
            bmax = mc if bmax is None else jnp.maximum(bmax, mc)
            if kb_c is not None:
                p = jnp.exp2(s_c[c, :] - m_new).astype(jnp.bfloat16)
                start = kb_c * tk + c.start
                if not isinstance(start, int):
                    start = pl.multiple_of(start, sub)
                v = v_ref[0, :, pl.ds(start, sub)]
                t = jnp.dot(v, p, preferred_element_type=f32)
                pv = t if pv is None else pv + t
        b_a[...] = bmax
        if kb_c is not None:
            acc_ref[...] = alpha * acc_ref[...] + pv

    def finish_tile(tile):
        acc = acc_ref[...]
        start = tile * tq
        if not isinstance(start, int):
            start = pl.multiple_of(start, tq)
        o_ref[0, :, pl.ds(start, tq)] = acc[0:HEAD_V] / acc[HEAD_V:HEAD_V + 1]
        acc_ref[...] = jnp.zeros_like(acc_ref)

    @pl.when(stream_start)
    def _():
        acc_ref[...] = jnp.zeros_like(acc_ref)
        m_ref[...] = jnp.full_like(m_ref, NEG_INF)
        region(qq_ref[0], lambda c: k_ref[0, 0, c, :], None, 1, None)

    def block(t, carry):
        tile = jax.lax.div(t, nk)
        kb = t - tile * nk
        last = kb == nk - 1
        for parity in (0, 1):
            @pl.when(t % 2 == parity)
            def _():
                q_a = qq_ref[tile + jnp.where(last, 1, 0)]
                kb_a = jnp.where(last, 0, kb + 1)
                m_old = jnp.where(kb == 0, NEG_INF, m_ref[...])
                region(q_a, lambda c: k_ref[0, kb_a, c, :], kb, parity, m_old)

        @pl.when(last)
        def _():
            finish_tile(tile)
        return carry

    jax.lax.fori_loop(0, n_tiles * nk - 1, block, 0)
    region(qq_ref[n_tiles], lambda c: kn_ref[0, 0, c, :], nk - 1, 1, m_ref[...])
    finish_tile(n_tiles - 1)


def _dense_attention(qT, kT, vT, maps_per_value, name):
    M, d, S = qT.shape
    tq = min(DENSE_TQ, S)
    tk = min(DENSE_TK, S)
    nk, nq = S // tk, S // tq
    tiles = min(DENSE_TILES_PER_STEP, nq)
    assert nk >= 2 and nk % 2 == 0 and nq % tiles == 0
    k = jnp.swapaxes(kT, 1, 2).reshape(M, nk, tk, d)
    row = pltpu.VMEM((1, tq), jnp.float32)
    steps = nq // tiles

    def after(m, i):
        same = i + 1 < steps
        return jnp.where(same, m, jnp.minimum(m + 1, M - 1)), jnp.where(same, i + 1, 0)

    def qn_index(m, i):
        m2, i2 = after(m, i)
        return m2, 0, tiles * i2

    def kn_index(m, i):
        return after(m, i)[0], 0, 0, 0

    return pl.pallas_call(
        _dense_kernel,
        grid=(M, steps),
        in_specs=[pl.BlockSpec((1, d, tiles * tq), lambda m, i: (m, 0, i)),
                  pl.BlockSpec((1, d, tq), qn_index),
                  pl.BlockSpec((1, nk, tk, d), lambda m, i: (m, 0, 0, 0)),
                  pl.BlockSpec((1, 1, tk, d), kn_index),
                  pl.BlockSpec((1, V_AUG, S), lambda m, i: (m // maps_per_value, 0, 0))],
        out_specs=pl.BlockSpec((1, HEAD_V, tiles * tq), lambda m, i: (m, 0, i)),
        out_shape=jax.ShapeDtypeStruct((M, HEAD_V, S), jnp.float32),
        scratch_shapes=[pltpu.VMEM((V_AUG, tq), jnp.float32), row,
                        pltpu.VMEM((tk, tq), jnp.float32), pltpu.VMEM((tk, tq), jnp.float32),
                        row, row, pltpu.VMEM((tiles + 1, d, tq), qT.dtype)],
        compiler_params=_params(2),
        name=name,
    )(qT, qT, k, k, vT)


def _local_kernel(sink_ref, *refs, groups):
    f32 = jnp.float32
    n_in, n_g = 8, len(groups)
    outs = refs[n_in * n_g:(n_in + 1) * n_g]
    scratch = refs[(n_in + 1) * n_g:]
    j = pl.program_id(0)
    n_heads = max(refs[n_in * g].shape[0] for g in range(n_g))

    @pl.when(j == 0)
    def _():
        for g in range(n_g):
            scratch[4 * g + 1][...] = jnp.zeros_like(scratch[4 * g + 1])
            scratch[4 * g + 3][...] = jnp.zeros_like(scratch[4 * g + 3])

    def step(parity):
        for h in range(n_heads):
            for g, (has_sink, q_per_kv) in enumerate(groups):
                q_ref, k0_ref, k1_ref, k2_ref, v0_ref, v1_ref, v2_ref, b_ref = refs[n_in * g:n_in * (g + 1)]
                if h >= q_ref.shape[0]:
                    continue
                s_a, s_b = scratch[4 * g + parity], scratch[4 * g + 1 - parity]
                m_a, m_b = scratch[4 * g + 2 + parity], scratch[4 * g + 3 - parity]
                blk = q_ref.shape[2]
                chunks = [slice(c * blk, (c + 1) * blk) for c in range(LOCAL_CHUNKS)]
                kv = h // q_per_kv
                q = q_ref[h]
                m = None
                for c, k in zip(chunks, (k0_ref, k1_ref, k2_ref)):
                    s = jnp.dot(k[kv], q, preferred_element_type=f32) + b_ref[0, h % b_ref.shape[1], c]
                    s_a[h, c] = s
                    mc = jnp.max(s, axis=0, keepdims=True)
                    m = mc if m is None else jnp.maximum(m, mc)
                if has_sink:
                    m = jnp.maximum(m, sink_ref[h])
                m_a[h] = m
                m = m_b[h]
                pv = None
                for c, v in zip(chunks, (v0_ref, v1_ref, v2_ref)):
                    p = jnp.exp2(s_b[h, c] - m).astype(jnp.bfloat16)
                    t = jnp.dot(v[kv], p, preferred_element_type=f32)
                    pv = t if pv is None else pv + t
                l = pv[HEAD_V:HEAD_V + 1]
                if has_sink:
                    l = l + jnp.exp2(sink_ref[h] - m)
                outs[g][h] = pv[0:HEAD_V] / l

    for parity in (0, 1):
        @pl.when(j % 2 == parity)
        def _():
            step(parity)


def _local_attention(sinks, group_args):
    S = group_args[0][0].shape[2]
    blk = LOCAL_BLK
    nb = S // blk
    assert nb >= LOCAL_CHUNKS

    def base(i):
        return jnp.clip(i - 1, 0, nb - LOCAL_CHUNKS)

    def case(i):
        return jnp.where(i == 0, 0, jnp.where(i == nb - 1, 2, 1))

    def scored(j):
        return jnp.minimum(j, nb - 1)

    def finished(j):
        return jnp.maximum(j - 1, 0)

    operands, in_specs, out_specs, out_shapes, groups, scratch = (
        [sinks], [pl.BlockSpec(memory_space=pltpu.SMEM)], [], [], [], [])
    for qT, kT, vT, bias, q_per_kv, has_sink in group_args:
        Hq, d, _ = qT.shape
        Hk, Hb = kT.shape[0], bias.shape[1]
        k = jnp.swapaxes(kT, 1, 2)
        operands += [qT, k, k, k, vT, vT, vT, bias]
        in_specs.append(pl.BlockSpec((Hq, d, blk), lambda j: (0, 0, scored(j))))
        in_specs += [pl.BlockSpec((Hk, blk, d),
                                  functools.partial(lambda c, j: (0, base(scored(j)) + c, 0), c))
                     for c in range(LOCAL_CHUNKS)]
        in_specs += [pl.BlockSpec((Hk, V_AUG, blk),
                                  functools.partial(lambda c, j: (0, 0, base(finished(j)) + c), c))
                     for c in range(LOCAL_CHUNKS)]
        in_specs.append(pl.BlockSpec((1, Hb, LOCAL_CHUNKS * blk, blk),
                                     lambda j: (case(scored(j)), 0, 0, 0)))
        out_specs.append(pl.BlockSpec((Hq, HEAD_V, blk), lambda j: (0, 0, finished(j))))
        out_shapes.append(jax.ShapeDtypeStruct((Hq, HEAD_V, S), jnp.float32))
        groups.append((has_sink, q_per_kv))
        scratch += [pltpu.VMEM((Hq, LOCAL_CHUNKS * blk, blk), jnp.float32)] * 2
        scratch += [pltpu.VMEM((Hq, 1, blk), jnp.float32)] * 2
    return pl.pallas_call(
        functools.partial(_local_kernel, groups=tuple(groups)),
        grid=(nb + 1,),
        in_specs=in_specs,
        out_specs=out_specs,
        out_shape=out_shapes,
        scratch_shapes=scratch,
        compiler_params=_params(1),
        name="local",
    )(*operands)


def _swa_bias(nb):
    blk = LOCAL_BLK
    kk = np.arange(LOCAL_CHUNKS * blk)[:, None]
    qq = np.arange(blk)[None, :]
    out = []
    for q_off in (0, blk, 2 * blk):
        valid = np.abs(kk - (q_off + qq)) <= WINDOW
        out.append(np.where(valid, 0.0, NEG_INF))
    return jnp.asarray(np.stack(out)[:, None], jnp.float32)


def _na_bias(rpb, S):
    blk = LOCAL_BLK
    rows = S // GRID_W
    rpq = blk // GRID_W
    kpq = LOCAL_CHUNKS * rpq
    nb = S // blk
    n_off = 2 * NA_ROWS - 1
    kc = np.arange(GRID_W)[:, None]
    w = np.arange(GRID_W)[None, :]
    cs = np.clip(w - NA_COLS // 2, 0, GRID_W - NA_COLS)
    col_ok = (kc >= cs) & (kc < cs + NA_COLS)
    onehot = (kc - w + NA_COLS - 1)[None] == np.arange(2 * NA_COLS - 1)[:, None, None]
    onehot = jnp.asarray(onehot.reshape(2 * NA_COLS - 1, GRID_W * GRID_W), jnp.float32)
    block_of = np.full((3, kpq, rpq), n_off, np.int32)
    for c, i in enumerate((0, 1, nb - 1)):
        base_row = rpq * int(np.clip(i - 1, 0, nb - LOCAL_CHUNKS))
        for ki in range(kpq):
            for qj in range(rpq):
                r, kr = rpq * i + qj, base_row + ki
                rs = int(np.clip(r - NA_ROWS // 2, 0, rows - NA_ROWS))
                if rs <= kr < rs + NA_ROWS:
                    block_of[c, ki, qj] = kr - r + NA_ROWS - 1
    L, H = rpb.shape[:2]
    toe = jnp.einsum("lhab,bn->lhan", rpb, onehot, precision=jax.lax.Precision.HIGHEST)
    toe = toe.reshape(L, H, n_off, GRID_W, GRID_W) * LOG2E
    toe = jnp.where(jnp.asarray(col_ok), toe, NEG_INF)
    toe = jnp.concatenate([toe, jnp.full((L, H, 1, GRID_W, GRID_W), NEG_INF, jnp.float32)], axis=2)
    t = jnp.take(toe, jnp.asarray(block_of.reshape(-1)), axis=2)
    t = t.reshape(L, H, 3, kpq, rpq, GRID_W, GRID_W)
    t = jnp.transpose(t, (0, 2, 1, 3, 5, 4, 6))
    return t.reshape(L, 3, H, kpq * GRID_W, rpq * GRID_W)


def _post_kernel(x_ref, oa_ref, ob_ref, oc_ref, od_ref, ga_ref, gc_ref, gd_ref, gsub_ref,
                 lq1_ref, lk1_ref, lq2_ref, lk2_ref, linit_ref, w_out_ref, g2_ref, w_up_ref,
                 w_down_ref, gf_ref, y_ref, *, final):
    bf16 = jnp.bfloat16
    f32 = jnp.float32
    tokens = x_ref.shape[1]

    def heads(ref):
        return ref[...].reshape(ref.shape[0] * ref.shape[1], tokens)

    ya = _rms_rows(heads(oa_ref), ga_ref[...])
    yc = _rms_rows(heads(oc_ref), gc_ref[...])
    yd = _rms_rows(heads(od_ref), gd_ref[...])
    lam_init = linit_ref[...]
    lam = (jnp.exp(jnp.sum(lq1_ref[...] * lk1_ref[...], axis=1, keepdims=True))
           - jnp.exp(jnp.sum(lq2_ref[...] * lk2_ref[...], axis=1, keepdims=True)) + lam_init)
    yb = []
    for hd in range(DIFF_HEADS):
        w = ob_ref[2 * hd] - lam * ob_ref[2 * hd + 1]
        yb.append(_rms_rows(w, gsub_ref[...]) * (1.0 - lam_init))
    mix = jnp.concatenate([ya] + yb + [yc, yd], axis=0).astype(bf16)
    x1 = x_ref[...] + jnp.dot(w_out_ref[...], mix, preferred_element_type=f32)
    h2 = _rms_rows(x1, g2_ref[...]).astype(bf16)
    u = jnp.dot(w_up_ref[...], h2, preferred_element_type=f32)
    a = jnp.square(jnp.maximum(u, 0.0)).astype(bf16)
    x2 = x1 + jnp.dot(w_down_ref[...], a, preferred_element_type=f32)
    if final:
        x2 = _rms_rows(x2, gf_ref[...])
    y_ref[...] = x2


def _post(xT, oa, ob, oc, od, ga, gc, gd, gsub, lq1, lk1, lq2, lk2, linit, w_outT, g2, w_upT,
          w_downT, gf, final):
    S = xT.shape[1]
    tt = min(TOKEN_TILE, S)

    def tok(rows):
        return pl.BlockSpec((rows, tt), lambda i: (0, i))

    def heads(arr):
        return pl.BlockSpec((arr.shape[0], arr.shape[1], tt), lambda i: (0, 0, i))

    def resident(arr):
        nd = arr.ndim
        return pl.BlockSpec(arr.shape, lambda *_: (0,) * nd, pipeline_mode=pl.Buffered(1))

    small = [ga, gc, gd, gsub, lq1, lk1, lq2, lk2, linit]
    return pl.pallas_call(
        functools.partial(_post_kernel, final=final),
        grid=(S // tt,),
        in_specs=[tok(D_MODEL), heads(oa), heads(ob), heads(oc), heads(od)]
                 + [_const_spec(a.shape) for a in small]
                 + [resident(w_outT), _const_spec(g2.shape), resident(w_upT), resident(w_downT),
                    _const_spec(gf.shape)],
        out_specs=tok(D_MODEL),
        out_shape=jax.ShapeDtypeStruct((D_MODEL, S), jnp.float32),
        compiler_params=_params(1),
        name="post_final" if final else "post",
    )(xT, oa, ob, oc, od, *small, w_outT, g2, w_upT, w_downT, gf)


def _rope_tables_T(S, dim):
    inv = 1.0 / (ROPE_THETA ** (jnp.arange(0, dim, 2, dtype=jnp.float32) / dim))
    ang = inv[:, None] * jnp.arange(S, dtype=jnp.float32)[None, :]
    return jnp.cos(ang), jnp.sin(ang)


def _col(v):
    return v.astype(jnp.float32)[:, None]


def _wT(w):
    return jnp.swapaxes(w, -1, -2).astype(jnp.bfloat16)


@jax.jit
def _forward(x, norm1_g, w_in, mla_q_norm_g, mla_w_uq, mla_kv_norm_g, mla_w_uk, mla_w_uv,
             diff_lambda_q1, diff_lambda_k1, diff_lambda_q2, diff_lambda_k2, diff_subln_g,
             swa_sinks, na_rpb, out_g_mla, out_g_swa, out_g_na, w_out, norm2_g, w_up, w_down,
             final_norm_g):
    B, S, D = x.shape
    assert B == 1 and D == D_MODEL and S % TOKEN_TILE == 0 and S % GRID_W == 0
    c32, s32 = _rope_tables_T(S, MLA_ROPE_DIM)
    c64, s64 = _rope_tables_T(S, SWA_HEAD_DIM)
    swa_bias = _swa_bias(S // LOCAL_BLK)
    na_bias = _na_bias(na_rpb.astype(jnp.float32), S)
    xT = x[0].T
    gf = _col(final_norm_g)
    for l in range(DEPTH):
        prep = _prep(xT, _col(norm1_g[l]), _wT(w_in[l]), _col(mla_q_norm_g[l]), _wT(mla_w_uq[l]),
                     _col(mla_kv_norm_g[l]), _wT(mla_w_uk[l]), _wT(mla_w_uv[l]), c32, s32, c64, s64)
        mq, mk, mv, dq, dk, dv, sq, sk, sv, nq, nk, nv = prep
        oa = _dense_attention(mq, mk, mv, 1, "dense_mla")
        ob = _dense_attention(dq, dk, dv, 2, "dense_diff")
        oc, od = _local_attention(swa_sinks[l].astype(jnp.float32) * LOG2E,
                                  [(sq, sk, sv, swa_bias, SWA_GROUP, True),
                                   (nq, nk, nv, na_bias[l], 1, False)])
        lam_init = 0.8 - 0.6 * math.exp(-0.3 * l)
        row = lambda v: v.astype(jnp.float32)[None, :]
        xT = _post(xT, oa, ob, oc, od, _col(out_g_mla[l]), _col(out_g_swa[l]), _col(out_g_na[l]),
                   _col(diff_subln_g[l]), row(diff_lambda_q1[l]), row(diff_lambda_k1[l]),
                   row(diff_lambda_q2[l]), row(diff_lambda_k2[l]),
                   jnp.full((1, 1), lam_init, jnp.float32), _wT(w_out[l]), _col(norm2_g[l]),
                   _wT(w_up[l]), _wT(w_down[l]), gf, final=(l == DEPTH - 1))
    return xT.T[None]


def kernel(x, norm1_g, w_in, mla_q_norm_g, mla_w_uq, mla_kv_norm_g, mla_w_uk, mla_w_uv,
           diff_lambda_q1, diff_lambda_k1, diff_lambda_q2, diff_lambda_k2, diff_subln_g,
           swa_sinks, na_rpb, out_g_mla, out_g_swa, out_g_na, w_out, norm2_g, w_up, w_down,
           final_norm_g):
    return _forward(x, norm1_g, w_in, mla_q_norm_g, mla_w_uq, mla_kv_norm_g, mla_w_uk, mla_w_uv,
                    diff_lambda_q1, diff_lambda_k1, diff_lambda_q2, diff_lambda_k2, diff_subln_g,
                    swa_sinks, na_rpb, out_g_mla, out_g_swa, out_g_na, w_out, norm2_g, w_up,
                    w_down, final_norm_g)
```

```python
import functools
import math

import jax
import jax.numpy as jnp
import numpy as np
from jax.experimental import pallas as pl
from jax.experimental.pallas import tpu as pltpu

D_MODEL = 1024
DEPTH = 4
GRID_W = 64
ROPE_THETA = 10000.0
EPS = 1e-6
NEG_INF = -1e30

MLA_HEADS = 4
MLA_Q_RANK = 192
MLA_KV_RANK = 128
MLA_NOPE_DIM = 64
MLA_ROPE_DIM = 32
MLA_QK_DIM = MLA_NOPE_DIM + MLA_ROPE_DIM
DIFF_HEADS = 4
DIFF_QK_DIM = 32
SWA_Q_HEADS = 4
SWA_KV_HEADS = 2
SWA_GROUP = SWA_Q_HEADS // SWA_KV_HEADS
SWA_HEAD_DIM = 64
WINDOW = 128
NA_HEADS = 4
NA_HEAD_DIM = 64
NA_ROWS = 8
NA_COLS = 16
HEAD_V = 64

IN_SPLIT_WIDTHS = (
    MLA_Q_RANK, MLA_KV_RANK, MLA_ROPE_DIM,
    DIFF_HEADS * 2 * DIFF_QK_DIM, DIFF_HEADS * 2 * DIFF_QK_DIM, DIFF_HEADS * HEAD_V,
    SWA_Q_HEADS * SWA_HEAD_DIM, SWA_KV_HEADS * SWA_HEAD_DIM, SWA_KV_HEADS * SWA_HEAD_DIM,
    NA_HEADS * NA_HEAD_DIM, NA_HEADS * NA_HEAD_DIM, NA_HEADS * NA_HEAD_DIM,
)
IN_COLS = sum(IN_SPLIT_WIDTHS)
IN_OFFSETS = tuple(int(v) for v in np.cumsum((0,) + IN_SPLIT_WIDTHS))

V7X_BF16_SUBLANES = 16
V7X_VMEM_LIMIT_BYTES = 56 * 1024 * 1024

LOG2E = math.log2(math.e)
V_AUG = HEAD_V + V7X_BF16_SUBLANES

TOKEN_TILE = 512
DENSE_TQ = 512
DENSE_TILES_PER_STEP = 4
DENSE_TK = 8192
DENSE_SUB = 256
LOCAL_BLK = 256
LOCAL_CHUNKS = 3


def _params(n_axes):
    return pltpu.CompilerParams(
        dimension_semantics=("arbitrary",) * n_axes,
        vmem_limit_bytes=V7X_VMEM_LIMIT_BYTES,
    )


def _const_spec(shape):
    nd = len(shape)
    return pl.BlockSpec(shape, lambda *_: (0,) * nd)


def _rms_rows(x, g):
    r = jax.lax.rsqrt(jnp.mean(x * x, axis=0, keepdims=True) + EPS)
    return (x * r) * g


def _rope_rows(x1, x2, c, s):
    return x1 * c - x2 * s, x2 * c + x1 * s


def _ones_rows(tokens):
    row = jax.lax.broadcasted_iota(jnp.int32, (V7X_BF16_SUBLANES, tokens), 0)
    return jnp.where(row == 0, 1.0, 0.0).astype(jnp.bfloat16)


def _prep_kernel(x_ref, g1_ref, w_in_ref, qg_ref, w_uq_ref, kvg_ref, w_uk_ref, w_uv_ref,
                 c32_ref, s32_ref, c64_ref, s64_ref,
                 mq_ref, mk_ref, mv_ref, dq_ref, dk_ref, dv_ref,
                 sq_ref, sk_ref, sv_ref, nq_ref, nk_ref, nv_ref):
    bf16 = jnp.bfloat16
    f32 = jnp.float32
    tokens = x_ref.shape[1]
    h = _rms_rows(x_ref[...], g1_ref[...]).astype(bf16)
    proj = jnp.dot(w_in_ref[...], h, preferred_element_type=f32)
    part = [proj[IN_OFFSETS[i]:IN_OFFSETS[i + 1]] for i in range(len(IN_SPLIT_WIDTHS))]
    a_cq, a_ckv, a_kr, b_q, b_k, b_v, c_q, c_k, c_v, d_q, d_k, d_v = part
    c32, s32 = c32_ref[...], s32_ref[...]
    c64, s64 = c64_ref[...], s64_ref[...]
    ones = _ones_rows(tokens)

    mla_scale = MLA_QK_DIM ** -0.5 * LOG2E
    cqn = _rms_rows(a_cq, qg_ref[...]).astype(bf16)
    q_all = jnp.dot(w_uq_ref[...], cqn, preferred_element_type=f32)
    lat = _rms_rows(a_ckv, kvg_ref[...]).astype(bf16)
    k_nope = jnp.dot(w_uk_ref[...], lat, preferred_element_type=f32)
    v_all = jnp.dot(w_uv_ref[...], lat, preferred_element_type=f32)
    half = MLA_ROPE_DIM // 2
    kp1, kp2 = _rope_rows(a_kr[:half], a_kr[half:], c32, s32)
    for hd in range(MLA_HEADS):
        q = q_all[hd * MLA_QK_DIM:(hd + 1) * MLA_QK_DIM]
        q1, q2 = _rope_rows(q[MLA_NOPE_DIM:MLA_NOPE_DIM + half], q[MLA_NOPE_DIM + half:], c32, s32)
        mq_ref[hd, 0:MLA_NOPE_DIM] = (q[:MLA_NOPE_DIM] * mla_scale).astype(bf16)
        mq_ref[hd, MLA_NOPE_DIM:MLA_NOPE_DIM + half] = (q1 * mla_scale).astype(bf16)
        mq_ref[hd, MLA_NOPE_DIM + half:MLA_QK_DIM] = (q2 * mla_scale).astype(bf16)
        mk_ref[hd, 0:MLA_NOPE_DIM] = k_nope[hd * MLA_NOPE_DIM:(hd + 1) * MLA_NOPE_DIM].astype(bf16)
        mk_ref[hd, MLA_NOPE_DIM:MLA_NOPE_DIM + half] = kp1.astype(bf16)
        mk_ref[hd, MLA_NOPE_DIM + half:MLA_QK_DIM] = kp2.astype(bf16)
        mv_ref[hd, 0:HEAD_V] = v_all[hd * HEAD_V:(hd + 1) * HEAD_V].astype(bf16)
        mv_ref[hd, HEAD_V:V_AUG] = ones

    diff_scale = DIFF_QK_DIM ** -0.5 * LOG2E
    half = DIFF_QK_DIM // 2
    for m in range(2 * DIFF_HEADS):
        lo = m * DIFF_QK_DIM
        q1, q2 = _rope_rows(b_q[lo:lo + half], b_q[lo + half:lo + DIFF_QK_DIM], c32, s32)
        k1, k2 = _rope_rows(b_k[lo:lo + half], b_k[lo + half:lo + DIFF_QK_DIM], c32, s32)
        dq_ref[m, 0:half] = (q1 * diff_scale).astype(bf16)
        dq_ref[m, half:DIFF_QK_DIM] = (q2 * diff_scale).astype(bf16)
        dk_ref[m, 0:half] = k1.astype(bf16)
        dk_ref[m, half:DIFF_QK_DIM] = k2.astype(bf16)
    for hd in range(DIFF_HEADS):
        dv_ref[hd, 0:HEAD_V] = b_v[hd * HEAD_V:(hd + 1) * HEAD_V].astype(bf16)
        dv_ref[hd, HEAD_V:V_AUG] = ones

    swa_scale = SWA_HEAD_DIM ** -0.5 * LOG2E
    half = SWA_HEAD_DIM // 2
    for hd in range(SWA_Q_HEADS):
        lo = hd * SWA_HEAD_DIM
        q1, q2 = _rope_rows(c_q[lo:lo + half], c_q[lo + half:lo + SWA_HEAD_DIM], c64, s64)
        sq_ref[hd, 0:half] = (q1 * swa_scale).astype(bf16)
        sq_ref[hd, half:SWA_HEAD_DIM] = (q2 * swa_scale).astype(bf16)
    for hd in range(SWA_KV_HEADS):
        lo = hd * SWA_HEAD_DIM
        k1, k2 = _rope_rows(c_k[lo:lo + half], c_k[lo + half:lo + SWA_HEAD_DIM], c64, s64)
        sk_ref[hd, 0:half] = k1.astype(bf16)
        sk_ref[hd, half:SWA_HEAD_DIM] = k2.astype(bf16)
        sv_ref[hd, 0:HEAD_V] = c_v[lo:lo + HEAD_V].astype(bf16)
        sv_ref[hd, HEAD_V:V_AUG] = ones

    na_scale = NA_HEAD_DIM ** -0.5 * LOG2E
    for hd in range(NA_HEADS):
        lo = hd * NA_HEAD_DIM
        nq_ref[hd] = (d_q[lo:lo + NA_HEAD_DIM] * na_scale).astype(bf16)
        nk_ref[hd] = d_k[lo:lo + NA_HEAD_DIM].astype(bf16)
        nv_ref[hd, 0:HEAD_V] = d_v[lo:lo + HEAD_V].astype(bf16)
        nv_ref[hd, HEAD_V:V_AUG] = ones


def _prep(xT, g1, w_inT, qg, w_uqT, kvg, w_ukT, w_uvT, c32, s32, c64, s64):
    S = xT.shape[1]
    tt = min(TOKEN_TILE, S)
    bf16 = jnp.bfloat16

    def tok(rows):
        return pl.BlockSpec((rows, tt), lambda i: (0, i))

    def head_out(heads, rows):
        return (jax.ShapeDtypeStruct((heads, rows, S), bf16),
                pl.BlockSpec((heads, rows, tt), lambda i: (0, 0, i)))

    outs = [head_out(MLA_HEADS, MLA_QK_DIM), head_out(MLA_HEADS, MLA_QK_DIM), head_out(MLA_HEADS, V_AUG),
            head_out(2 * DIFF_HEADS, DIFF_QK_DIM), head_out(2 * DIFF_HEADS, DIFF_QK_DIM),
            head_out(DIFF_HEADS, V_AUG),
            head_out(SWA_Q_HEADS, SWA_HEAD_DIM), head_out(SWA_KV_HEADS, SWA_HEAD_DIM),
            head_out(SWA_KV_HEADS, V_AUG),
            head_out(NA_HEADS, NA_HEAD_DIM), head_out(NA_HEADS, NA_HEAD_DIM), head_out(NA_HEADS, V_AUG)]
    return pl.pallas_call(
        _prep_kernel,
        grid=(S // tt,),
        in_specs=[tok(D_MODEL), _const_spec(g1.shape), _const_spec(w_inT.shape), _const_spec(qg.shape),
                  _const_spec(w_uqT.shape), _const_spec(kvg.shape), _const_spec(w_ukT.shape),
                  _const_spec(w_uvT.shape), tok(c32.shape[0]), tok(s32.shape[0]), tok(c64.shape[0]),
                  tok(s64.shape[0])],
        out_specs=[o[1] for o in outs],
        out_shape=[o[0] for o in outs],
        compiler_params=_params(1),
        name="prep",
    )(xT, g1, w_inT, qg, w_uqT, kvg, w_ukT, w_uvT, c32, s32, c64, s64)


def _dense_kernel(q_ref, qn_ref, k_ref, kn_ref, v_ref, o_ref, acc_ref, m_ref,
                  s0_ref, s1_ref, b0_ref, b1_ref, qq_ref):
    f32 = jnp.float32
    nk, tk = k_ref.shape[1], k_ref.shape[2]
    tq = qn_ref.shape[2]
    n_tiles = q_ref.shape[2] // tq
    stream_start = jnp.logical_and(pl.program_id(0) == 0, pl.program_id(1) == 0)
    for g in range(n_tiles):
        qq_ref[g] = q_ref[0, :, g * tq:(g + 1) * tq]
    qq_ref[n_tiles] = qn_ref[0]
    sub = min(DENSE_SUB, tk)
    subs = [slice(c * sub, (c + 1) * sub) for c in range(tk // sub)]
    bufs = ((s0_ref, b0_ref), (s1_ref, b1_ref))

    def region(q_a, k_a, kb_c, parity, m_old):
        s_a, b_a = bufs[1 - parity]
        s_c, b_c = bufs[parity]
        if kb_c is not None:
            m_new = jnp.maximum(m_old, b_c[...])
            alpha = jnp.exp2(m_old - m_new)
            m_ref[...] = m_new
        bmax = pv = None
        for c in subs:
            s = jnp.dot(k_a(c), q_a, preferred_element_type=f32)
            s_a[c, :] = s
            mc = jnp.max(s, axis=0, keepdims=True)
            bmax = mc if bmax is None else jnp.maximum(bmax, mc)
            if kb_c is not None:
                p = jnp.exp2(s_c[c, :] - m_new).astype(jnp.bfloat16)
                start = kb_c * tk + c.start
                if not isinstance(start, int):
                    start = pl.multiple_of(start, sub)
                v = v_ref[0, :, pl.ds(start, sub)]
                t = jnp.dot(v, p, preferred_element_type=f32)
                pv = t if pv is None else pv + t
        b_a[...] = bmax
        if kb_c is not None:
            acc_ref[...] = alpha * acc_ref[...] + pv

    def finish_tile(tile):
        acc = acc_ref[...]
        start = tile * tq
        if not isinstance(start, int):
            start = pl.multiple_of(start, tq)
        o_ref[0, :, pl.ds(start, tq)] = acc[0:HEAD_V] / acc[HEAD_V:HEAD_V + 1]
        acc_ref[...] = jnp.zeros_like(acc_ref)

    @pl.when(stream_start)
    def _():
        acc_ref[...] = jnp.zeros_like(acc_ref)
        m_ref[...] = jnp.full_like(m_ref, NEG_INF)
        region(qq_ref[0], lambda c: k_ref[0, 0, c, :], None, 1, None)

    def block(t, carry):
        tile = jax.lax.div(t, nk)
        kb = t - tile * nk
        last = kb == nk - 1
        for parity in (0, 1):
            @pl.when(t % 2 == parity)
            def _():
                q_a = qq_ref[tile + jnp.where(last, 1, 0)]
                kb_a = jnp.where(last, 0, kb + 1)
                m_old = jnp.where(kb == 0, NEG_INF, m_ref[...])
                region(q_a, lambda c: k_ref[0, kb_a, c, :], kb, parity, m_old)

        @pl.when(last)
        def _():
            finish_tile(tile)
        return carry

    jax.lax.fori_loop(0, n_tiles * nk - 1, block, 0)
    region(qq_ref[n_tiles], lambda c: kn_ref[0, 0, c, :], nk - 1, 1, m_ref[...])
    finish_tile(n_tiles - 1)


def _dense_attention(qT, kT, vT, maps_per_value, name):
    M, d, S = qT.shape
    tq = min(DENSE_TQ, S)
    tk = min(DENSE_TK, S)
    nk, nq = S // tk, S // tq
    tiles = min(DENSE_TILES_PER_STEP, nq)
    assert nk >= 2 and nk % 2 == 0 and nq % tiles == 0
    k = jnp.swapaxes(kT, 1, 2).reshape(M, nk, tk, d)
    row = pltpu.VMEM((1, tq), jnp.float32)
    steps = nq // tiles

    def after(m, i):
        same = i + 1 < steps
        return jnp.where(same, m, jnp.minimum(m + 1, M - 1)), jnp.where(same, i + 1, 0)

    def qn_index(m, i):
        m2, i2 = after(m, i)
        return m2, 0, tiles * i2

    def kn_index(m, i):
        return after(m, i)[0], 0, 0, 0

    return pl.pallas_call(
        _dense_kernel,
        grid=(M, steps),
        in_specs=[pl.BlockSpec((1, d, tiles * tq), lambda m, i: (m, 0, i)),
                  pl.BlockSpec((1, d, tq), qn_index),
                  pl.BlockSpec((1, nk, tk, d), lambda m, i: (m, 0, 0, 0)),
                  pl.BlockSpec((1, 1, tk, d), kn_index),
                  pl.BlockSpec((1, V_AUG, S), lambda m, i: (m // maps_per_value, 0, 0))],
        out_specs=pl.BlockSpec((1, HEAD_V, tiles * tq), lambda m, i: (m, 0, i)),
        out_shape=jax.ShapeDtypeStruct((M, HEAD_V, S), jnp.float32),
        scratch_shapes=[pltpu.VMEM((V_AUG, tq), jnp.float32), row,
                        pltpu.VMEM((tk, tq), jnp.float32), pltpu.VMEM((tk, tq), jnp.float32),
                        row, row, pltpu.VMEM((tiles + 1, d, tq), qT.dtype)],
        compiler_params=_params(2),
        name=name,
    )(qT, qT, k, k, vT)


def _local_kernel(sink_ref, *refs, groups):
    f32 = jnp.float32
    n_in, n_g = 8, len(groups)
    outs = refs[n_in * n_g:(n_in + 1) * n_g]
    scratch = refs[(n_in + 1) * n_g:]
    j = pl.program_id(0)
    n_heads = max(refs[n_in * g].shape[0] for g in range(n_g))

    @pl.when(j == 0)
    def _():
        for g in range(n_g):
            scratch[4 * g + 1][...] = jnp.zeros_like(scratch[4 * g + 1])
            scratch[4 * g + 3][...] = jnp.zeros_like(scratch[4 * g + 3])

    def step(parity):
        for h in range(n_heads):
            for g, (has_sink, q_per_kv) in enumerate(groups):
                q_ref, k0_ref, k1_ref, k2_ref, v0_ref, v1_ref, v2_ref, b_ref = refs[n_in * g:n_in * (g + 1)]
                if h >= q_ref.shape[0]:
                    continue
                s_a, s_b = scratch[4 * g + parity], scratch[4 * g + 1 - parity]
                m_a, m_b = scratch[4 * g + 2 + parity], scratch[4 * g + 3 - parity]
                blk = q_ref.shape[2]
                chunks = [slice(c * blk, (c + 1) * blk) for c in range(LOCAL_CHUNKS)]
                kv = h // q_per_kv
                q = q_ref[h]
                m = None
                for c, k in zip(chunks, (k0_ref, k1_ref, k2_ref)):
                    s = jnp.dot(k[kv], q, preferred_element_type=f32) + b_ref[0, h % b_ref.shape[1], c]
                    s_a[h, c] = s
                    mc = jnp.max(s, axis=0, keepdims=True)
                    m = mc if m is None else jnp.maximum(m, mc)
                if has_sink:
                    m = jnp.maximum(m, sink_ref[h])
                m_a[h] = m
                m = m_b[h]
                pv = None
                for c, v in zip(chunks, (v0_ref, v1_ref, v2_ref)):
                    p = jnp.exp2(s_b[h, c] - m).astype(jnp.bfloat16)
                    t = jnp.dot(v[kv], p, preferred_element_type=f32)
                    pv = t if pv is None else pv + t
                l = pv[HEAD_V:HEAD_V + 1]
                if has_sink:
                    l = l + jnp.exp2(sink_ref[h] - m)
                outs[g][h] = pv[0:HEAD_V] / l

    for parity in (0, 1):
        @pl.when(j % 2 == parity)
        def _():
            step(parity)


def _local_attention(sinks, group_args):
    S = group_args[0][0].shape[2]
    blk = LOCAL_BLK
    nb = S // blk
    assert nb >= LOCAL_CHUNKS

    def base(i):
        return jnp.clip(i - 1, 0, nb - LOCAL_CHUNKS)

    def case(i):
        return jnp.where(i == 0, 0, jnp.where(i == nb - 1, 2, 1))

    def scored(j):
        return jnp.minimum(j, nb - 1)

    def finished(j):
        return jnp.maximum(j - 1, 0)

    operands, in_specs, out_specs, out_shapes, groups, scratch = (
        [sinks], [pl.BlockSpec(memory_space=pltpu.SMEM)], [], [], [], [])
    for qT, kT, vT, bias, q_per_kv, has_sink in group_args:
        Hq, d, _ = qT.shape
        Hk, Hb = kT.shape[0], bias.shape[1]
        k = jnp.swapaxes(kT, 1, 2)
        operands += [qT, k, k, k, vT, vT, vT, bias]
        in_specs.append(pl.BlockSpec((Hq, d, blk), lambda j: (0, 0, scored(j))))
        in_specs += [pl.BlockSpec((Hk, blk, d),
                                  functools.partial(lambda c, j: (0, base(scored(j)) + c, 0), c))
                     for c in range(LOCAL_CHUNKS)]
        in_specs += [pl.BlockSpec((Hk, V_AUG, blk),
                                  functools.partial(lambda c, j: (0, 0, base(finished(j)) + c), c))
                     for c in range(LOCAL_CHUNKS)]
        in_specs.append(pl.BlockSpec((1, Hb, LOCAL_CHUNKS * blk, blk),
                                     lambda j: (case(scored(j)), 0, 0, 0)))
        out_specs.append(pl.BlockSpec((Hq, HEAD_V, blk), lambda j: (0, 0, finished(j))))
        out_shapes.append(jax.ShapeDtypeStruct((Hq, HEAD_V, S), jnp.float32))
        groups.append((has_sink, q_per_kv))
        scratch += [pltpu.VMEM((Hq, LOCAL_CHUNKS * blk, blk), jnp.float32)] * 2
        scratch += [pltpu.VMEM((Hq, 1, blk), jnp.float32)] * 2
    return pl.pallas_call(
        functools.partial(_local_kernel, groups=tuple(groups)),
        grid=(nb + 1,),
        in_specs=in_specs,
        out_specs=out_specs,
        out_shape=out_shapes,
        scratch_shapes=scratch,
        compiler_params=_params(1),
        name="local",
    )(*operands)


def _swa_bias(nb):
    blk = LOCAL_BLK
    kk = np.arange(LOCAL_CHUNKS * blk)[:, None]
    qq = np.arange(blk)[None, :]
    out = []
    for q_off in (0, blk, 2 * blk):
        valid = np.abs(kk - (q_off + qq)) <= WINDOW
        out.append(np.where(valid, 0.0, NEG_INF))
    return jnp.asarray(np.stack(out)[:, None], jnp.float32)


def _na_bias(rpb, S):
    blk = LOCAL_BLK
    rows = S // GRID_W
    rpq = blk // GRID_W
    kpq = LOCAL_CHUNKS * rpq
    nb = S // blk
    n_off = 2 * NA_ROWS - 1
    kc = np.arange(GRID_W)[:, None]
    w = np.arange(GRID_W)[None, :]
    cs = np.clip(w - NA_COLS // 2, 0, GRID_W - NA_COLS)
    col_ok = (kc >= cs) & (kc < cs + NA_COLS)
    onehot = (kc - w + NA_COLS - 1)[None] == np.arange(2 * NA_COLS - 1)[:, None, None]
    onehot = jnp.asarray(onehot.reshape(2 * NA_COLS - 1, GRID_W * GRID_W), jnp.float32)
    block_of = np.full((3, kpq, rpq), n_off, np.int32)
    for c, i in enumerate((0, 1, nb - 1)):
        base_row = rpq * int(np.clip(i - 1, 0, nb - LOCAL_CHUNKS))
        for ki in range(kpq):
            for qj in range(rpq):
                r, kr = rpq * i + qj, base_row + ki
                rs = int(np.clip(r - NA_ROWS // 2, 0, rows - NA_ROWS))
                if rs <= kr < rs + NA_ROWS:
                    block_of[c, ki, qj] = kr - r + NA_ROWS - 1
    L, H = rpb.shape[:2]
    toe = jnp.einsum("lhab,bn->lhan", rpb, onehot, precision=jax.lax.Precision.HIGHEST)
    toe = toe.reshape(L, H, n_off, GRID_W, GRID_W) * LOG2E
    toe = jnp.where(jnp.asarray(col_ok), toe, NEG_INF)
    toe = jnp.concatenate([toe, jnp.full((L, H, 1, GRID_W, GRID_W), NEG_INF, jnp.float32)], axis=2)
    t = jnp.take(toe, jnp.asarray(block_of.reshape(-1)), axis=2)
    t = t.reshape(L, H, 3, kpq, rpq, GRID_W, GRID_W)
    t = jnp.transpose(t, (0, 2, 1, 3, 5, 4, 6))
    return t.reshape(L, 3, H, kpq * GRID_W, rpq * GRID_W)


def _post_kernel(x_ref, oa_ref, ob_ref, oc_ref, od_ref, ga_ref, gc_ref, gd_ref, gsub_ref,
                 lq1_ref, lk1_ref, lq2_ref, lk2_ref, linit_ref, w_out_ref, g2_ref, w_up_ref,
                 w_down_ref, gf_ref, y_ref, *, final):
    bf16 = jnp.bfloat16
    f32 = jnp.float32
    tokens = x_ref.shape[1]

    def heads(ref):
        return ref[...].reshape(ref.shape[0] * ref.shape[1], tokens)

    ya = _rms_rows(heads(oa_ref), ga_ref[...])
    yc = _rms_rows(heads(oc_ref), gc_ref[...])
    yd = _rms_rows(heads(od_ref), gd_ref[...])
    lam_init = linit_ref[...]
    lam = (jnp.exp(jnp.sum(lq1_ref[...] * lk1_ref[...], axis=1, keepdims=True))
           - jnp.exp(jnp.sum(lq2_ref[...] * lk2_ref[...], axis=1, keepdims=True)) + lam_init)
    yb = []
    for hd in range(DIFF_HEADS):
        w = ob_ref[2 * hd] - lam * ob_ref[2 * hd + 1]
        yb.append(_rms_rows(w, gsub_ref[...]) * (1.0 - lam_init))
    mix = jnp.concatenate([ya] + yb + [yc, yd], axis=0).astype(bf16)
    x1 = x_ref[...] + jnp.dot(w_out_ref[...], mix, preferred_element_type=f32)
    h2 = _rms_rows(x1, g2_ref[...]).astype(bf16)
    u = jnp.dot(w_up_ref[...], h2, preferred_element_type=f32)
    a = jnp.square(jnp.maximum(u, 0.0)).astype(bf16)
    x2 = x1 + jnp.dot(w_down_ref[...], a, preferred_element_type=f32)
    if final:
        x2 = _rms_rows(x2, gf_ref[...]).T
    y_ref[...] = x2


def _post(xT, oa, ob, oc, od, ga, gc, gd, gsub, lq1, lk1, lq2, lk2, linit, w_outT, g2, w_upT,
          w_downT, gf, final):
    S = xT.shape[1]
    tt = min(TOKEN_TILE, S)

    def tok(rows):
        return pl.BlockSpec((rows, tt), lambda i: (0, i))

    def heads(arr):
        return pl.BlockSpec((arr.shape[0], arr.shape[1], tt), lambda i: (0, 0, i))

    def resident(arr):
        nd = arr.ndim
        return pl.BlockSpec(arr.shape, lambda *_: (0,) * nd, pipeline_mode=pl.Buffered(1))

    small = [ga, gc, gd, gsub, lq1, lk1, lq2, lk2, linit]
    return pl.pallas_call(
        functools.partial(_post_kernel, final=final),
        grid=(S // tt,),
        in_specs=[tok(D_MODEL), heads(oa), heads(ob), heads(oc), heads(od)]
                 + [_const_spec(a.shape) for a in small]
                 + [resident(w_outT), _const_spec(g2.shape), resident(w_upT), resident(w_downT),
                    _const_spec(gf.shape)],
        out_specs=pl.BlockSpec((tt, D_MODEL), lambda i: (i, 0)) if final else tok(D_MODEL),
        out_shape=jax.ShapeDtypeStruct((S, D_MODEL) if final else (D_MODEL, S), jnp.float32),
        compiler_params=_params(1),
        name="post_final" if final else "post",
    )(xT, oa, ob, oc, od, *small, w_outT, g2, w_upT, w_downT, gf)


def _rope_tables_T(S, dim):
    inv = 1.0 / (ROPE_THETA ** (jnp.arange(0, dim, 2, dtype=jnp.float32) / dim))
    ang = inv[:, None] * jnp.arange(S, dtype=jnp.float32)[None, :]
    return jnp.cos(ang), jnp.sin(ang)


def _col(v):
    return v.astype(jnp.float32)[:, None]


def _wT(w):
    return jnp.swapaxes(w, -1, -2).astype(jnp.bfloat16)


@jax.jit
def _forward(x, norm1_g, w_in, mla_q_norm_g, mla_w_uq, mla_kv_norm_g, mla_w_uk, mla_w_uv,
             diff_lambda_q1, diff_lambda_k1, diff_lambda_q2, diff_lambda_k2, diff_subln_g,
             swa_sinks, na_rpb, out_g_mla, out_g_swa, out_g_na, w_out, norm2_g, w_up, w_down,
             final_norm_g):
    B, S, D = x.shape
    assert B == 1 and D == D_MODEL and S % TOKEN_TILE == 0 and S % GRID_W == 0
    c32, s32 = _rope_tables_T(S, MLA_ROPE_DIM)
    c64, s64 = _rope_tables_T(S, SWA_HEAD_DIM)
    swa_bias = _swa_bias(S // LOCAL_BLK)
    na_bias = _na_bias(na_rpb.astype(jnp.float32), S)
    xT = x[0].T
    gf = _col(final_norm_g)
    for l in range(DEPTH):
        prep = _prep(xT, _col(norm1_g[l]), _wT(w_in[l]), _col(mla_q_norm_g[l]), _wT(mla_w_uq[l]),
                     _col(mla_kv_norm_g[l]), _wT(mla_w_uk[l]), _wT(mla_w_uv[l]), c32, s32, c64, s64)
        mq, mk, mv, dq, dk, dv, sq, sk, sv, nq, nk, nv = prep
        oa = _dense_attention(mq, mk, mv, 1, "dense_mla")
        ob = _dense_attention(dq, dk, dv, 2, "dense_diff")
        oc, od = _local_attention(swa_sinks[l].astype(jnp.float32) * LOG2E,
                                  [(sq, sk, sv, swa_bias, SWA_GROUP, True),
                                   (nq, nk, nv, na_bias[l], 1, False)])
        lam_init = 0.8 - 0.6 * math.exp(-0.3 * l)
        row = lambda v: v.astype(jnp.float32)[None, :]
        xT = _post(xT, oa, ob, oc, od, _col(out_g_mla[l]), _col(out_g_swa[l]), _col(out_g_na[l]),
                   _col(diff_subln_g[l]), row(diff_lambda_q1[l]), row(diff_lambda_k1[l]),
                   row(diff_lambda_q2[l]), row(diff_lambda_k2[l]),
                   jnp.full((1, 1), lam_init, jnp.float32), _wT(w_out[l]), _col(norm2_g[l]),
                   _wT(w_up[l]), _wT(w_down[l]), gf, final=(l == DEPTH - 1))
    return xT[None]


def kernel(x, norm1_g, w_in, mla_q_norm_g, mla_w_uq, mla_kv_norm_g, mla_w_uk, mla_w_uv,
           diff_lambda_q1, diff_lambda_k1, diff_lambda_q2, diff_lambda_k2, diff_subln_g,
           swa_sinks, na_rpb, out_g_mla, out_g_swa, out_g_na, w_out, norm2_g, w_up, w_down,
           final_norm_g):
    return _forward(x, norm1_g, w_in, mla_q_norm_g, mla_w_uq, mla_kv_norm_g, mla_w_uk, mla_w_uv,
                    diff_lambda_q1, diff_lambda_k1, diff_lambda_q2, diff_lambda_k2, diff_subln_g,
                    swa_sinks, na_rpb, out_g_mla, out_g_swa, out_g_na, w_out, norm2_g, w_up,
                    w_down, final_norm_g)
```
